```python
import math
import jax, jax.numpy as jnp
from jax import lax
import numpy as np

D_MODEL = 4096
BATCH = 2
SEQ = 8192
DEPTH = 4

A_HEADS = 16
A_HEAD_DIM = 128
IDX_HEADS = 32
IDX_DIM = 64
TOPK_MAX = 256
B_HEADS = 16
QK_NOPE_DIM = 128
QK_ROPE_DIM = 64
V_HEAD_DIM = 128
Q_LORA_RANK = 1024
KV_LORA_RANK = 512
ROPE_THETA = 10000.0
SGU_WIDTH = 2 * D_MODEL
SGU_GROUPS = 16
CHUNK = 128
QBLOCK = 128
EPS = 1e-6
A_WIDTH = A_HEADS * A_HEAD_DIM
B_WIDTH = B_HEADS * V_HEAD_DIM
ATT_WIDTH = A_WIDTH + B_WIDTH
ATT_SPLIT_SIZES = (
    A_WIDTH,
    A_HEAD_DIM,
    A_HEAD_DIM,
    IDX_HEADS * IDX_DIM,
    IDX_DIM,
    IDX_HEADS,
    Q_LORA_RANK,
    KV_LORA_RANK,
    QK_ROPE_DIM,
    ATT_WIDTH,
)
ATT_IN_COLS = sum(ATT_SPLIT_SIZES)
N_ATT_LAYERS = (DEPTH + 1) // 2
N_SGU_LAYERS = DEPTH // 2

kernel_name = "hybrid_dsa_mla_chunked_sgu_trunk"


def _rmsnorm(x, g):
    xf = x.astype(jnp.float32)
    y = xf * lax.rsqrt(jnp.mean(xf * xf, axis=-1, keepdims=True) + EPS)
    return (y * g.astype(jnp.float32)).astype(x.dtype)


def _layernorm(x, g, b):
    xf = x.astype(jnp.float32)
    mu = jnp.mean(xf, axis=-1, keepdims=True)
    var = jnp.mean(jnp.square(xf - mu), axis=-1, keepdims=True)
    y = (xf - mu) * lax.rsqrt(var + EPS)
    return (y * g.astype(jnp.float32) + b.astype(jnp.float32)).astype(x.dtype)


def _rope_tables(positions, dtype):
    inv_freq = ROPE_THETA ** (-jnp.arange(0, QK_ROPE_DIM, 2, dtype=jnp.float32) / QK_ROPE_DIM)
    ang = positions.astype(jnp.float32)[..., None] * inv_freq
    return jnp.cos(ang)[:, :, None, :].astype(dtype), jnp.sin(ang)[:, :, None, :].astype(dtype)


def _apply_rope(x, cos, sin):
    half = x.shape[-1] // 2
    x1, x2 = x[..., :half], x[..., half:]
    return jnp.concatenate([x1 * cos - x2 * sin, x2 * cos + x1 * sin], axis=-1)


def _sweep_blocks(block_fn, n_blocks, batch, seq):
    out = lax.map(block_fn, jnp.arange(n_blocks))
    out = jnp.moveaxis(out, 0, 1)
    return out.reshape(batch, seq, *out.shape[3:])


def _dsa_attention(q, k, v, qi, ki, wi, topk):
    B, S = q.shape[0], q.shape[1]
    kpos = jnp.arange(S)
    bidx = jnp.arange(B)[:, None, None]
    scale = 1.0 / math.sqrt(A_HEAD_DIM)

    def block(i):
        start = i * QBLOCK
        qb = lax.dynamic_slice_in_dim(q, start, QBLOCK, axis=1)
        qib = lax.dynamic_slice_in_dim(qi, start, QBLOCK, axis=1)
        wib = lax.dynamic_slice_in_dim(wi, start, QBLOCK, axis=1).astype(jnp.float32)
        qpos = start + jnp.arange(QBLOCK)
        causal = kpos[None, :] <= qpos[:, None]
        logits = jnp.einsum('bqhd,bkd->bqhk', qib, ki).astype(jnp.float32)
        score = jnp.einsum('bqhk,bqh->bqk', jax.nn.relu(logits), wib)
        score = jnp.where(causal[None], score, -jnp.inf)
        _, idx = lax.top_k(score, topk)
        valid = idx <= qpos[None, :, None]
        kg = k[bidx, idx]
        vg = v[bidx, idx]
        s = jnp.einsum('bqhd,bqkd->bqhk', qb, kg).astype(jnp.float32) * scale
        s = jnp.where(valid[:, :, None, :], s, -jnp.inf)
        p = jax.nn.softmax(s, axis=-1).astype(vg.dtype)
        return jnp.einsum('bqhk,bqkd->bqhd', p, vg)

    return _sweep_blocks(block, S // QBLOCK, B, S)


def _mla_attention(q_nope, q_rope, k_nope, k_rope, v):
    B, S = q_nope.shape[0], q_nope.shape[1]
    kpos = jnp.arange(S)
    scale = 1.0 / math.sqrt(QK_NOPE_DIM + QK_ROPE_DIM)

    def block(i):
        start = i * QBLOCK
        qn = lax.dynamic_slice_in_dim(q_nope, start, QBLOCK, axis=1)
        qr = lax.dynamic_slice_in_dim(q_rope, start, QBLOCK, axis=1)
        s = (jnp.einsum('bqhd,bkhd->bhqk', qn, k_nope)
             + jnp.einsum('bqhr,bkr->bhqk', qr, k_rope)).astype(jnp.float32) * scale
        qpos = start + jnp.arange(QBLOCK)
        causal = kpos[None, :] <= qpos[:, None]
        s = jnp.where(causal[None, None], s, -jnp.inf)
        p = jax.nn.softmax(s, axis=-1).astype(v.dtype)
        return jnp.einsum('bhqk,bkhd->bqhd', p, v)

    return _sweep_blocks(block, S // QBLOCK, B, S)


def _att_mixer(h, cos, sin, w_in, q_norm_g, kv_norm_g, w_uq, w_ukv, topk):
    B, S, _ = h.shape
    proj = h @ w_in
    cuts = [int(c) for c in np.cumsum(ATT_SPLIT_SIZES)[:-1]]
    q_a, k_a, v_a, qi, ki, wi, c_q, c_kv, k_rope, gate = jnp.split(proj, cuts, axis=-1)
    q_a = q_a.reshape(B, S, A_HEADS, A_HEAD_DIM)
    qi = qi.reshape(B, S, IDX_HEADS, IDX_DIM)
    wi = wi * (IDX_HEADS ** -0.5 * IDX_DIM ** -0.5)
    o_a = _dsa_attention(q_a, k_a, v_a, qi, ki, wi, topk)
    q = (_rmsnorm(c_q, q_norm_g) @ w_uq).reshape(B, S, B_HEADS, QK_NOPE_DIM + QK_ROPE_DIM)
    q_nope, q_rope = q[..., :QK_NOPE_DIM], _apply_rope(q[..., QK_NOPE_DIM:], cos, sin)
    kv = (_rmsnorm(c_kv, kv_norm_g) @ w_ukv).reshape(B, S, B_HEADS, QK_NOPE_DIM + V_HEAD_DIM)
    k_nope, v_b = kv[..., :QK_NOPE_DIM], kv[..., QK_NOPE_DIM:]
    k_rope = _apply_rope(k_rope[:, :, None, :], cos, sin)[:, :, 0, :]
    o_b = _mla_attention(q_nope, q_rope, k_nope, k_rope, v_b)
    o = jnp.concatenate([o_a.reshape(B, S, A_WIDTH), o_b.reshape(B, S, B_WIDTH)], axis=-1)
    return o * jax.nn.silu(gate)


def _sgu_mixer(h, w_in, ln_g, ln_b, w_s, b_s):
    B, S, _ = h.shape
    u, v, gate = jnp.split(h @ w_in, 3, axis=-1)
    u, v = jax.nn.gelu(u), jax.nn.gelu(v)
    v = _layernorm(v, ln_g, ln_b)
    v = v.reshape(B, S // CHUNK, CHUNK, SGU_GROUPS, SGU_WIDTH // SGU_GROUPS)
    tril = jnp.tril(jnp.ones((CHUNK, CHUNK), dtype=w_s.dtype))
    ws = w_s * tril[None]
    mixed = jnp.einsum('gts,bnsgc->bntgc', ws, v) + b_s.T[None, None, :, :, None]
    y = u * mixed.reshape(B, S, SGU_WIDTH)
    return y * jax.nn.silu(gate)


def setup_inputs(seed: int = 0) -> dict:
    key = jax.random.key(seed)
    ks = jax.random.split(key, 16)
    f32 = jnp.float32

    def w(k, shape, fan_in):
        return jax.random.normal(k, shape, f32) * (fan_in ** -0.5)

    def gain(k, shape):
        return 1.0 + 0.01 * jax.random.normal(k, shape, f32)

    out_scale = (2.0 * DEPTH) ** -0.5
    x = jax.random.normal(ks[0], (BATCH, SEQ, D_MODEL), f32)
    positions = jnp.broadcast_to(jnp.arange(SEQ, dtype=jnp.int32)[None, :], (BATCH, SEQ))
    return {
        "x": x,
        "positions": positions,
        "norm_g": gain(ks[1], (DEPTH, D_MODEL)),
        "final_norm_g": gain(ks[2], (D_MODEL,)),
        "att_w_in": w(ks[3], (N_ATT_LAYERS, D_MODEL, ATT_IN_COLS), D_MODEL),
        "att_q_norm_g": gain(ks[4], (N_ATT_LAYERS, Q_LORA_RANK)),
        "att_kv_norm_g": gain(ks[5], (N_ATT_LAYERS, KV_LORA_RANK)),
        "att_w_uq": w(ks[6], (N_ATT_LAYERS, Q_LORA_RANK, B_HEADS * (QK_NOPE_DIM + QK_ROPE_DIM)), Q_LORA_RANK),
        "att_w_ukv": w(ks[7], (N_ATT_LAYERS, KV_LORA_RANK, B_HEADS * (QK_NOPE_DIM + V_HEAD_DIM)), KV_LORA_RANK),
        "att_w_out": w(ks[8], (N_ATT_LAYERS, ATT_WIDTH, D_MODEL), ATT_WIDTH) * out_scale,
        "sgu_w_in": w(ks[9], (N_SGU_LAYERS, D_MODEL, 3 * SGU_WIDTH), D_MODEL),
        "sgu_ln_g": gain(ks[10], (N_SGU_LAYERS, SGU_WIDTH)),
        "sgu_ln_b": 0.01 * jax.random.normal(ks[11], (N_SGU_LAYERS, SGU_WIDTH), f32),
        "sgu_w_s": w(ks[12], (N_SGU_LAYERS, SGU_GROUPS, CHUNK, CHUNK), CHUNK),
        "sgu_b_s": gain(ks[13], (N_SGU_LAYERS, SGU_GROUPS, CHUNK)),
        "sgu_w_out": w(ks[14], (N_SGU_LAYERS, SGU_WIDTH, D_MODEL), SGU_WIDTH) * out_scale,
    }


def reference(x, positions, norm_g, final_norm_g, att_w_in, att_q_norm_g, att_kv_norm_g,
              att_w_uq, att_w_ukv, att_w_out, sgu_w_in, sgu_ln_g, sgu_ln_b, sgu_w_s,
              sgu_b_s, sgu_w_out):
    seq = x.shape[1]
    topk = min(TOPK_MAX, seq // 4)
    cos, sin = _rope_tables(positions, x.dtype)
    for layer in range(DEPTH):
        h = _rmsnorm(x, norm_g[layer])
        j = layer // 2
        if layer % 2 == 0:
            y = _att_mixer(h, cos, sin, att_w_in[j], att_q_norm_g[j], att_kv_norm_g[j],
                           att_w_uq[j], att_w_ukv[j], topk)
            x = x + y @ att_w_out[j]
        else:
            y = _sgu_mixer(h, sgu_w_in[j], sgu_ln_g[j], sgu_ln_b[j], sgu_w_s[j], sgu_b_s[j])
            x = x + y @ sgu_w_out[j]
    return _rmsnorm(x, final_norm_g)
```

```python
import functools
import math

import jax
import jax.numpy as jnp
from jax import lax
from jax.experimental import pallas as pl
from jax.experimental.pallas import tpu as pltpu

A_HEADS = 16
A_HEAD_DIM = 128
IDX_HEADS = 32
IDX_DIM = 64
TOPK_MAX = 256
B_HEADS = 16
QK_NOPE_DIM = 128
QK_ROPE_DIM = 64
V_HEAD_DIM = 128
ROPE_THETA = 10000.0
SGU_GROUPS = 16
CHUNK = 128
EPS = 1e-6

LANES = 128
V7X_VMEM_BYTES = 64 * 1024 * 1024
VMEM_BUDGET = V7X_VMEM_BYTES - 8 * 1024 * 1024

MLA_QK_PAD = 2 * LANES
NEG_BIG = -1e30
INT_MIN = -(2 ** 31)
LOG2E = math.log2(math.e)

DSA_TQ = 128
DSA_TK = 512
MLA_T = 512

F32 = jnp.float32
BF16 = jnp.bfloat16


def _cparams(sem, vmem_bytes):
    limit = int(min(max(vmem_bytes, 16 * 1024 * 1024), VMEM_BUDGET))
    return pltpu.CompilerParams(dimension_semantics=sem, vmem_limit_bytes=limit)


def _pick(n, candidates):
    for c in candidates:
        if n % c == 0:
            return c
    raise ValueError(f"no tile in {candidates} divides {n}")


def _nt_dot(a, b):
    return lax.dot_general(a, b, (((1,), (1,)), ((), ())), preferred_element_type=F32)


def _rmsnorm_kernel(x_ref, g_ref, o_ref):
    x = x_ref[...]
    ms = jnp.mean(x * x, axis=-1, keepdims=True)
    o_ref[...] = (x * lax.rsqrt(ms + EPS) * g_ref[...]).astype(o_ref.dtype)


def _rmsnorm(x, g, out_dtype):
    n, d = x.shape
    tm = _pick(n, (256, 128, 64, 32, 16, 8))
    return pl.pallas_call(
        _rmsnorm_kernel,
        out_shape=jax.ShapeDtypeStruct((n, d), out_dtype),
        grid=(n // tm,),
        in_specs=[pl.BlockSpec((tm, d), lambda i: (i, 0)),
                  pl.BlockSpec((1, d), lambda i: (0, 0))],
        out_specs=pl.BlockSpec((tm, d), lambda i: (i, 0)),
        compiler_params=_cparams(("parallel",), 6 * tm * d * 4),
        name="rmsnorm",
    )(x, g.reshape(1, d))


def _mm_kernel(*refs, nk, n_extra, epilogue):
    a_ref, b_ref = refs[0], refs[1]
    extra = refs[2:2 + n_extra]
    o_ref = refs[2 + n_extra]

    def finish(acc):
        vals = [e[...] for e in extra]
        o_ref[...] = epilogue(acc, *vals).astype(o_ref.dtype)

    if nk == 1:
        finish(jnp.dot(a_ref[...], b_ref[...], preferred_element_type=F32))
        return

    acc_ref = refs[3 + n_extra]
    k = pl.program_id(2)

    @pl.when(k == 0)
    def _():
        acc_ref[...] = jnp.zeros_like(acc_ref)

    acc_ref[...] += jnp.dot(a_ref[...], b_ref[...], preferred_element_type=F32)

    @pl.when(k == nk - 1)
    def _():
        finish(acc_ref[...])


def _identity_epilogue(acc):
    return acc


def _residual_epilogue(acc, x):
    return x + acc


def _gelu_epilogue(acc):
    return jax.nn.gelu(acc)


def _silu_epilogue(acc):
    return jax.nn.silu(acc)


def _matmul(a, b, out_dtype, *, epilogue=_identity_epilogue, extras=(), name="matmul"):
    m, kdim = a.shape
    n = b.shape[1]
    tm = _pick(m, (1024, 512, 256, 128))
    tn = _pick(n, (512, 256, 128))
    tk = _pick(kdim, (4096, 2048, 1024, 512, 256, 128))
    nk = kdim // tk
    out_bytes = jnp.dtype(out_dtype).itemsize
    vmem = 2 * (tm * tk * 2 + tk * tn * 2 + tm * tn * out_bytes)
    vmem += sum(2 * tm * tn * e.dtype.itemsize for e in extras)
    vmem += 3 * tm * tn * 4
    scratch = [pltpu.VMEM((tm, tn), F32)] if nk > 1 else []
    return pl.pallas_call(
        functools.partial(_mm_kernel, nk=nk, n_extra=len(extras), epilogue=epilogue),
        out_shape=jax.ShapeDtypeStruct((m, n), out_dtype),
        grid=(m // tm, n // tn, nk),
        in_specs=[pl.BlockSpec((tm, tk), lambda i, j, k: (i, k)),
                  pl.BlockSpec((tk, tn), lambda i, j, k: (k, j))]
                 + [pl.BlockSpec((tm, tn), lambda i, j, k: (i, j)) for _ in extras],
        out_specs=pl.BlockSpec((tm, tn), lambda i, j, k: (i, j)),
        scratch_shapes=scratch,
        compiler_params=_cparams(("parallel", "parallel", "arbitrary"), vmem),
        name=name,
    )(a, b, *extras)


def _att_out_kernel(ya_ref, yb_ref, wa_ref, wb_ref, x_ref, o_ref):
    acc = jnp.dot(ya_ref[...], wa_ref[...], preferred_element_type=F32)
    acc += jnp.dot(yb_ref[...], wb_ref[...], preferred_element_type=F32)
    o_ref[...] = x_ref[...] + acc


def _att_out(ya, yb, wa, wb, x):
    n, ka = ya.shape
    kb = yb.shape[1]
    d = wa.shape[1]
    tm = _pick(n, (1024, 512, 256, 128))
    tn = _pick(d, (512, 256, 128))
    vmem = 2 * (tm * (ka + kb) * 2 + (ka + kb) * tn * 2 + 2 * tm * tn * 4) + 3 * tm * tn * 4
    return pl.pallas_call(
        _att_out_kernel,
        out_shape=jax.ShapeDtypeStruct((n, d), F32),
        grid=(n // tm, d // tn),
        in_specs=[pl.BlockSpec((tm, ka), lambda i, j: (i, 0)),
                  pl.BlockSpec((tm, kb), lambda i, j: (i, 0)),
                  pl.BlockSpec((ka, tn), lambda i, j: (0, j)),
                  pl.BlockSpec((kb, tn), lambda i, j: (0, j)),
                  pl.BlockSpec((tm, tn), lambda i, j: (i, j))],
        out_specs=pl.BlockSpec((tm, tn), lambda i, j: (i, j)),
        compiler_params=_cparams(("parallel", "parallel"), vmem),
        name="att_out",
    )(ya, yb, wa, wb, x)


def _rope_lanes(v, tc, tsm, tsp):
    half = QK_ROPE_DIM // 2
    return (v * tc
            + pltpu.roll(v, LANES - half, axis=1) * tsm
            + pltpu.roll(v, half, axis=1) * tsp)


def _mla_q_kernel(c_ref, g_ref, w_ref, tc_ref, tsm_ref, tsp_ref, o_ref, *, heads):
    c = c_ref[...]
    ms = jnp.mean(c * c, axis=-1, keepdims=True)
    cn = (c * lax.rsqrt(ms + EPS) * g_ref[...]).astype(BF16)
    q = jnp.dot(cn, w_ref[...], preferred_element_type=F32)
    tc, tsm, tsp = tc_ref[...], tsm_ref[...], tsp_ref[...]
    for h in range(heads):
        base = h * MLA_QK_PAD
        o_ref[:, base:base + LANES] = q[:, base:base + LANES].astype(BF16)
        r = q[:, base + LANES:base + 2 * LANES]
        o_ref[:, base + LANES:base + 2 * LANES] = _rope_lanes(r, tc, tsm, tsp).astype(BF16)


def _mla_q(proj32, col_block, g, w, tc, tsm, tsp):
    n = proj32.shape[0]
    rank, width = w.shape
    heads = width // MLA_QK_PAD
    tm = _pick(n, (512, 256, 128))
    vmem = 2 * (tm * rank * 4 + rank * width * 2 + 3 * tm * LANES * 4 + tm * width * 2) + 2 * tm * width * 4
    row = lambda i: (i, 0)
    return pl.pallas_call(
        functools.partial(_mla_q_kernel, heads=heads),
        out_shape=jax.ShapeDtypeStruct((n, width), BF16),
        grid=(n // tm,),
        in_specs=[pl.BlockSpec((tm, rank), lambda i: (i, col_block)),
                  pl.BlockSpec((1, rank), lambda i: (0, 0)),
                  pl.BlockSpec((rank, width), lambda i: (0, 0)),
                  pl.BlockSpec((tm, LANES), row),
                  pl.BlockSpec((tm, LANES), row),
                  pl.BlockSpec((tm, LANES), row)],
        out_specs=pl.BlockSpec((tm, width), row),
        compiler_params=_cparams(("parallel",), vmem),
        name="mla_q",
    )(proj32, g.reshape(1, rank), w, tc, tsm, tsp)


def _mla_kv_kernel(c_ref, g_ref, w_ref, kr_ref, tc_ref, tsm_ref, tsp_ref, k_ref, v_ref, *, heads):
    c = c_ref[...]
    ms = jnp.mean(c * c, axis=-1, keepdims=True)
    cn = (c * lax.rsqrt(ms + EPS) * g_ref[...]).astype(BF16)
    kv = jnp.dot(cn, w_ref[...], preferred_element_type=F32)
    kr = _rope_lanes(kr_ref[...], tc_ref[...], tsm_ref[...], tsp_ref[...]).astype(BF16)
    for h in range(heads):
        base = h * MLA_QK_PAD
        k_ref[:, base:base + LANES] = kv[:, h * LANES:(h + 1) * LANES].astype(BF16)
        k_ref[:, base + LANES:base + 2 * LANES] = kr
    v_ref[...] = kv[:, heads * LANES:].astype(BF16)


def _mla_kv(proj32, c_block, kr_block, g, w, tc, tsm, tsp):
    n = proj32.shape[0]
    rank, width = w.shape
    heads = width // (2 * LANES)
    tm = _pick(n, (512, 256, 128))
    vmem = 2 * (tm * rank * 4 + rank * width * 2 + 4 * tm * LANES * 4 + tm * heads * 3 * LANES * 2)
    vmem += 2 * tm * width * 4
    row = lambda i: (i, 0)
    return pl.pallas_call(
        functools.partial(_mla_kv_kernel, heads=heads),
        out_shape=(jax.ShapeDtypeStruct((n, heads * MLA_QK_PAD), BF16),
                   jax.ShapeDtypeStruct((n, heads * LANES), BF16)),
        grid=(n // tm,),
        in_specs=[pl.BlockSpec((tm, rank), lambda i: (i, c_block)),
                  pl.BlockSpec((1, rank), lambda i: (0, 0)),
                  pl.BlockSpec((rank, width), lambda i: (0, 0)),
                  pl.BlockSpec((tm, LANES), lambda i: (i, kr_block)),
                  pl.BlockSpec((tm, LANES), row),
                  pl.BlockSpec((tm, LANES), row),
                  pl.BlockSpec((tm, LANES), row)],
        out_specs=(pl.BlockSpec((tm, heads * MLA_QK_PAD), row),
                   pl.BlockSpec((tm, heads * LANES), row)),
        compiler_params=_cparams(("parallel",), vmem),
        name="mla_kv",
    )(proj32, g.reshape(1, rank), w, proj32, tc, tsm, tsp)


def _mla_flash_kernel(q_ref, k_ref, v_ref, gate_ref, o_ref, m_ref, l_ref, acc_ref, *, t, scale):
    i = pl.program_id(2)
    j = pl.program_id(3)

    @pl.when(j == 0)
    def _():
        m_ref[...] = jnp.full_like(m_ref, NEG_BIG)
        l_ref[...] = jnp.zeros_like(l_ref)
        acc_ref[...] = jnp.zeros_like(acc_ref)

    def step(masked):
        s = _nt_dot(q_ref[...], k_ref[...])
        if masked:
            row = lax.broadcasted_iota(jnp.int32, (t, t), 0)
            col = lax.broadcasted_iota(jnp.int32, (t, t), 1)
            s = jnp.where(col <= row, s, NEG_BIG)
        m_old = m_ref[...]
        m_new = jnp.maximum(m_old, jnp.max(s, axis=-1, keepdims=True))
        alpha = jnp.exp2((m_old - m_new) * scale)
        p = jnp.exp2((s - m_new) * scale)
        l_ref[...] = alpha * l_ref[...] + jnp.sum(p, axis=-1, keepdims=True)
        acc_ref[...] = alpha * acc_ref[...] + jnp.dot(p.astype(BF16), v_ref[...],
                                                      preferred_element_type=F32)
        m_ref[...] = m_new

    @pl.when(j < i)
    def _():
        step(False)

    @pl.when(j == i)
    def _():
        step(True)
        o = acc_ref[...] / l_ref[...]
        o_ref[...] = (o * jax.nn.silu(gate_ref[...])).astype(o_ref.dtype)


def _mla_flash(q, k, v, proj32, gate_block0, batch, seq):
    heads = v.shape[1] // LANES
    t = _pick(seq, (MLA_T, 256, 128))
    nt = seq // t
    scale = LOG2E / math.sqrt(QK_NOPE_DIM + QK_ROPE_DIM)
    vmem = 2 * (2 * t * MLA_QK_PAD * 2 + t * LANES * 2 + t * LANES * 4 + t * LANES * 2)
    vmem += 3 * t * LANES * 4 + 6 * t * t * 4
    return pl.pallas_call(
        functools.partial(_mla_flash_kernel, t=t, scale=scale),
        out_shape=jax.ShapeDtypeStruct((batch * seq, heads * LANES), BF16),
        grid=(batch, heads, nt, nt),
        in_specs=[pl.BlockSpec((t, MLA_QK_PAD), lambda b, h, i, j: (b * nt + i, h)),
                  pl.BlockSpec((t, MLA_QK_PAD), lambda b, h, i, j: (b * nt + jnp.minimum(j, i), h)),
                  pl.BlockSpec((t, LANES), lambda b, h, i, j: (b * nt + jnp.minimum(j, i), h)),
                  pl.BlockSpec((t, LANES), lambda b, h, i, j: (b * nt + i, gate_block0 + h))],
        out_specs=pl.BlockSpec((t, LANES), lambda b, h, i, j: (b * nt + i, h)),
        scratch_shapes=[pltpu.VMEM((t, 1), F32), pltpu.VMEM((t, 1), F32), pltpu.VMEM((t, LANES), F32)],
        compiler_params=_cparams(("parallel", "parallel", "parallel", "arbitrary"), vmem),
        name="mla_flash",
    )(q, k, v, proj32)


def _order_key(score):
    score = jnp.where(score == 0.0, 0.0, score)
    bits = pltpu.bitcast(score, jnp.int32)
    return jnp.where(bits >= 0, bits, bits ^ jnp.int32(0x7FFFFFFF))


def _dsa_kernel(qa_ref, qi_ref, wi_ref, gate_ref, ka_ref, va_ref, k1_ref, k2_ref, o_ref,
                keys_ref, qall_ref, m_ref, l_ref, acc_ref, tri_ref,
                *, tq, tk, heads, idx_heads, topk, wi_scale, scale):
    i = pl.program_id(1)
    t0 = i * tq
    n_chunks = (t0 + tq + tk - 1) // tk
    row_pos = t0 + lax.broadcasted_iota(jnp.int32, (tq, tk), 0)
    col_iota = lax.broadcasted_iota(jnp.int32, (tq, tk), 1)

    w = wi_ref[...] * wi_scale

    def score_chunk(c, carry):
        off = pl.multiple_of(c * tk, tk)
        k1 = k1_ref[pl.ds(off, tk), :]
        k2 = k2_ref[pl.ds(off, tk), :]
        acc = jnp.zeros((tq, tk), F32)
        for p in range(idx_heads // 2):
            qp = qi_ref[:, p * LANES:(p + 1) * LANES]
            acc += jnp.maximum(_nt_dot(qp, k1), 0.0) * w[:, 2 * p:2 * p + 1]
            acc += jnp.maximum(_nt_dot(qp, k2), 0.0) * w[:, 2 * p + 1:2 * p + 2]
        key = jnp.where(off + col_iota <= row_pos, _order_key(acc), INT_MIN)
        keys_ref[:, pl.ds(off, tk)] = key
        return carry

    lax.fori_loop(0, n_chunks, score_chunk, 0)

    def count_ge(cand):
        def body(c, cnt):
            off = pl.multiple_of(c * tk, tk)
            ge = jnp.where(keys_ref[:, pl.ds(off, tk)] >= cand, 1.0, 0.0)
            for u in range(tk // LANES):
                cnt = cnt + ge[:, u * LANES:(u + 1) * LANES]
            return cnt
        cnt = lax.fori_loop(0, n_chunks, body, jnp.zeros((tq, LANES), F32))
        return jnp.sum(cnt, axis=-1, keepdims=True)

    kf = float(topk)
    thr = jnp.where(count_ge(jnp.zeros((tq, 1), jnp.int32)) >= kf, 0, INT_MIN).astype(jnp.int32)

    def bit_step(b, thr):
        cand = thr + jnp.left_shift(jnp.int32(1), 30 - b)
        return jnp.where(count_ge(cand) >= kf, cand, thr)

    thr = lax.fori_loop(0, 31, bit_step, thr)

    def count_gt_body(c, cnt):
        off = pl.multiple_of(c * tk, tk)
        gt = jnp.where(keys_ref[:, pl.ds(off, tk)] > thr, 1.0, 0.0)
        for u in range(tk // LANES):
            cnt = cnt + gt[:, u * LANES:(u + 1) * LANES]
        return cnt
    n_gt = jnp.sum(lax.fori_loop(0, n_chunks, count_gt_body, jnp.zeros((tq, LANES), F32)),
                   axis=-1, keepdims=True)
    need = jnp.where(thr == INT_MIN, 0.0, kf - n_gt)

    @pl.when(i == 0)
    def _():
        r = lax.broadcasted_iota(jnp.int32, (tk, tk), 0)
        cc = lax.broadcasted_iota(jnp.int32, (tk, tk), 1)
        tri_ref[...] = jnp.where(r <= cc, 1.0, 0.0).astype(tri_ref.dtype)

    for h in range(heads):
        qall_ref[h * tq:(h + 1) * tq, :] = qa_ref[:, h * LANES:(h + 1) * LANES]
    m_ref[...] = jnp.full_like(m_ref, NEG_BIG)
    l_ref[...] = jnp.zeros_like(l_ref)
    acc_ref[...] = jnp.zeros_like(acc_ref)

    def att_chunk(c, n_eq_before):
        off = pl.multiple_of(c * tk, tk)
        key = keys_ref[:, pl.ds(off, tk)]
        eq = key == thr
        rank = n_eq_before + jnp.dot(jnp.where(eq, 1.0, 0.0).astype(tri_ref.dtype), tri_ref[...],
                                     preferred_element_type=F32)
        sel = (key > thr) | (eq & (rank <= need))
        s = _nt_dot(qall_ref[...], ka_ref[pl.ds(off, tk), :])
        s = jnp.concatenate(
            [jnp.where(sel, s[h * tq:(h + 1) * tq], NEG_BIG) for h in range(heads)], axis=0)
        m_old = m_ref[...]
        m_new = jnp.maximum(m_old, jnp.max(s, axis=-1, keepdims=True))
        alpha = jnp.exp2((m_old - m_new) * scale)
        p = jnp.exp2((s - m_new) * scale)
        l_ref[...] = alpha * l_ref[...] + jnp.sum(p, axis=-1, keepdims=True)
        acc_ref[...] = alpha * acc_ref[...] + jnp.dot(p.astype(BF16), va_ref[pl.ds(off, tk), :],
                                                      preferred_element_type=F32)
        m_ref[...] = m_new
        return rank[:, tk - 1:tk]

    lax.fori_loop(0, n_chunks, att_chunk, jnp.zeros((tq, 1), F32))
    for h in range(heads):
        hs = slice(h * tq, (h + 1) * tq)
        g = gate_ref[:, h * LANES:(h + 1) * LANES]
        o = acc_ref[hs, :] / l_ref[hs, :]
        o_ref[:, h * LANES:(h + 1) * LANES] = (o * jax.nn.silu(g)).astype(o_ref.dtype)


def _dsa(proj16, proj32, lay16, lay32, batch, seq, topk):
    a_w = A_HEADS * A_HEAD_DIM
    idx_w = IDX_HEADS * IDX_DIM
    tq = _pick(seq, (DSA_TQ,))
    tk = _pick(seq, (DSA_TK, 256, 128))
    nq = seq // tq
    qa_b = lay16["q_a"] // a_w
    qi_b = lay16["qi"] // idx_w
    ka_b, va_b = lay16["k_a"] // LANES, lay16["v_a"] // LANES
    k1_b, k2_b = lay16["ki1"] // LANES, lay16["ki2"] // LANES
    wi_b = lay32["wi"] // LANES
    gate_b = lay32["gate"] // a_w
    vmem = 2 * (tq * a_w * 2 + tq * idx_w * 2 + tq * LANES * 4 + tq * a_w * 4 + 4 * seq * LANES * 2 + tq * a_w * 2)
    vmem += tq * seq * 4 + 8 * A_HEADS * tq * tk * 4
    qrow = lambda blk: (lambda b, i: (b * nq + i, blk))
    kcol = lambda blk: (lambda b, i: (b, blk))
    return pl.pallas_call(
        functools.partial(_dsa_kernel, tq=tq, tk=tk, heads=A_HEADS, idx_heads=IDX_HEADS, topk=topk,
                          wi_scale=IDX_HEADS ** -0.5 * IDX_DIM ** -0.5,
                          scale=LOG2E / math.sqrt(A_HEAD_DIM)),
        out_shape=jax.ShapeDtypeStruct((batch * seq, a_w), BF16),
        grid=(batch, nq),
        in_specs=[pl.BlockSpec((tq, a_w), qrow(qa_b)),
                  pl.BlockSpec((tq, idx_w), qrow(qi_b)),
                  pl.BlockSpec((tq, LANES), qrow(wi_b)),
                  pl.BlockSpec((tq, a_w), qrow(gate_b)),
                  pl.BlockSpec((seq, LANES), kcol(ka_b)),
                  pl.BlockSpec((seq, LANES), kcol(va_b)),
                  pl.BlockSpec((seq, LANES), kcol(k1_b)),
                  pl.BlockSpec((seq, LANES), kcol(k2_b))],
        out_specs=pl.BlockSpec((tq, a_w), lambda b, i: (b * nq + i, 0)),
        scratch_shapes=[pltpu.VMEM((tq, seq), jnp.int32),
                        pltpu.VMEM((A_HEADS * tq, LANES), BF16),
                        pltpu.VMEM((A_HEADS * tq, 1), F32),
                        pltpu.VMEM((A_HEADS * tq, 1), F32),
                        pltpu.VMEM((A_HEADS * tq, LANES), F32),
                        pltpu.VMEM((tk, tk), BF16)],
        compiler_params=_cparams(("parallel", "arbitrary"), vmem),
        name="dsa",
    )(proj16, proj16, proj32, proj32, proj16, proj16, proj16, proj16)


def _sgu_mix_kernel(u_ref, v_ref, sg_ref, lg_ref, lb_ref, ws_ref, bs_ref, o_ref, *, groups):
    v = v_ref[...]
    mu = jnp.mean(v, axis=-1, keepdims=True)
    d = v - mu
    var = jnp.mean(d * d, axis=-1, keepdims=True)
    vln = (d * lax.rsqrt(var + EPS) * lg_ref[...] + lb_ref[...]).astype(BF16)
    chunk, width = v.shape
    gw = width // groups
    row = lax.broadcasted_iota(jnp.int32, (chunk, chunk), 0)
    col = lax.broadcasted_iota(jnp.int32, (chunk, chunk), 1)
    bs = bs_ref[...]
    for g in range(groups):
        ws = jnp.where(col <= row, ws_ref[g], 0.0).astype(BF16)
        mixed = jnp.dot(ws, vln[:, g * gw:(g + 1) * gw], preferred_element_type=F32) + bs[:, g:g + 1]
        sl = slice(g * gw, (g + 1) * gw)
        y = u_ref[:, sl].astype(F32) * mixed * sg_ref[:, sl].astype(F32)
        o_ref[:, sl] = y.astype(o_ref.dtype)


def _sgu_mix(u, v, sg, ln_g, ln_b, w_s, b_s_t):
    n, width = v.shape
    groups = w_s.shape[0]
    vmem = 2 * (CHUNK * width * (2 + 4 + 2 + 2) + 2 * width * 4 + groups * CHUNK * CHUNK * 4) + 4 * CHUNK * width * 4
    row = lambda i: (i, 0)
    fixed2 = lambda i: (0, 0)
    return pl.pallas_call(
        functools.partial(_sgu_mix_kernel, groups=groups),
        out_shape=jax.ShapeDtypeStruct((n, width), BF16),
        grid=(n // CHUNK,),
        in_specs=[pl.BlockSpec((CHUNK, width), row),
                  pl.BlockSpec((CHUNK, width), row),
                  pl.BlockSpec((CHUNK, width), row),
                  pl.BlockSpec((1, width), fixed2),
                  pl.BlockSpec((1, width), fixed2),
                  pl.BlockSpec((groups, CHUNK, CHUNK), lambda i: (0, 0, 0)),
                  pl.BlockSpec((CHUNK, groups), fixed2)],
        out_specs=pl.BlockSpec((CHUNK, width), row),
        compiler_params=_cparams(("parallel",), vmem),
        name="sgu_mix",
    )(u, v, sg, ln_g.reshape(1, width), ln_b.reshape(1, width), w_s, b_s_t)


def _att_in_weights(w_in, q_rank, kv_rank):
    a_w, idx_w = A_HEADS * A_HEAD_DIM, IDX_HEADS * IDX_DIM
    att_w = a_w + B_HEADS * V_HEAD_DIM
    sizes = (a_w, A_HEAD_DIM, A_HEAD_DIM, idx_w, IDX_DIM, IDX_HEADS, q_rank, kv_rank, QK_ROPE_DIM, att_w)
    assert sum(sizes) == w_in.shape[1]
    names = ("q_a", "k_a", "v_a", "qi", "ki", "wi", "c_q", "c_kv", "k_rope", "gate")
    seg, off = {}, 0
    for nm, sz in zip(names, sizes):
        seg[nm] = w_in[:, off:off + sz]
        off += sz

    def pad_to(wseg, width, left=0):
        return jnp.pad(wseg, ((0, 0), (left, width - left - wseg.shape[1])))

    def pack(parts):
        parts = sorted(parts, key=lambda p: -p[1].shape[1])
        lay, off, cols = {}, 0, []
        for nm, wseg in parts:
            assert off % wseg.shape[1] == 0
            lay[nm] = off
            off += wseg.shape[1]
            cols.append(wseg)
        return jnp.concatenate(cols, axis=1).astype(BF16), lay

    w16, lay16 = pack([("q_a", seg["q_a"]), ("qi", seg["qi"]), ("k_a", seg["k_a"]), ("v_a", seg["v_a"]),
                       ("ki1", pad_to(seg["ki"], LANES)), ("ki2", pad_to(seg["ki"], LANES, left=IDX_DIM))])
    w32, lay32 = pack([("gate", seg["gate"]), ("c_q", seg["c_q"]), ("c_kv", seg["c_kv"]),
                       ("k_rope", pad_to(seg["k_rope"], LANES)), ("wi", pad_to(seg["wi"], LANES))])
    return w16, lay16, w32, lay32


def _rope_tables(positions):
    half = QK_ROPE_DIM // 2
    inv_freq = ROPE_THETA ** (-jnp.arange(0, QK_ROPE_DIM, 2, dtype=F32) / QK_ROPE_DIM)
    ang = positions.reshape(-1).astype(F32)[:, None] * inv_freq
    cos, sin = jnp.cos(ang), jnp.sin(ang)
    z = jnp.zeros_like(cos)
    pad = jnp.zeros((cos.shape[0], LANES - 2 * half), F32)
    tc = jnp.concatenate([cos, cos, pad], axis=1)
    tsm = jnp.concatenate([-sin, z, pad], axis=1)
    tsp = jnp.concatenate([z, sin, pad], axis=1)
    return tc, tsm, tsp


def _att_layer(x, g, w_in, q_norm_g, kv_norm_g, w_uq, w_ukv, w_out, tables, batch, seq, topk):
    tc, tsm, tsp = tables
    q_rank, kv_rank = q_norm_g.shape[0], kv_norm_g.shape[0]
    w16, lay16, w32, lay32 = _att_in_weights(w_in, q_rank, kv_rank)
    h = _rmsnorm(x, g, BF16)
    proj16 = _matmul(h, w16, BF16, name="att_in16")
    proj32 = _matmul(h, w32, F32, name="att_in32")

    ya = _dsa(proj16, proj32, lay16, lay32, batch, seq, topk)

    qk = QK_NOPE_DIM + QK_ROPE_DIM
    wq = jnp.pad(w_uq.reshape(q_rank, B_HEADS, qk), ((0, 0), (0, 0), (0, MLA_QK_PAD - qk)))
    wq = wq.reshape(q_rank, B_HEADS * MLA_QK_PAD).astype(BF16)
    wkv = w_ukv.reshape(kv_rank, B_HEADS, 2, LANES).transpose(0, 2, 1, 3).reshape(kv_rank, -1).astype(BF16)
    q = _mla_q(proj32, lay32["c_q"] // q_rank, q_norm_g, wq, tc, tsm, tsp)
    k, v = _mla_kv(proj32, lay32["c_kv"] // kv_rank, lay32["k_rope"] // LANES, kv_norm_g, wkv, tc, tsm, tsp)
    gate_b0 = (lay32["gate"] + A_HEADS * A_HEAD_DIM) // LANES
    yb = _mla_flash(q, k, v, proj32, gate_b0, batch, seq)

    a_w = A_HEADS * A_HEAD_DIM
    wo = w_out.astype(BF16)
    return _att_out(ya, yb, wo[:a_w], wo[a_w:], x)


def _sgu_layer(x, g, w_in, ln_g, ln_b, w_s, b_s, w_out):
    width = w_in.shape[1] // 3
    h = _rmsnorm(x, g, BF16)
    w = w_in.astype(BF16)
    u = _matmul(h, w[:, :width], BF16, epilogue=_gelu_epilogue, name="sgu_in_u")
    v = _matmul(h, w[:, width:2 * width], F32, epilogue=_gelu_epilogue, name="sgu_in_v")
    sg = _matmul(h, w[:, 2 * width:], BF16, epilogue=_silu_epilogue, name="sgu_in_gate")
    y = _sgu_mix(u, v, sg, ln_g, ln_b, w_s, b_s.T)
    return _matmul(y, w_out.astype(BF16), F32, epilogue=_residual_epilogue, extras=(x,), name="sgu_out")


def kernel(x, positions, norm_g, final_norm_g, att_w_in, att_q_norm_g, att_kv_norm_g, att_w_uq, att_w_ukv,
           att_w_out, sgu_w_in, sgu_ln_g, sgu_ln_b, sgu_w_s, sgu_b_s, sgu_w_out):
    batch, seq, d = x.shape
    depth = norm_g.shape[0]
    topk = min(TOPK_MAX, seq // 4)
    tables = _rope_tables(positions)
    xf = x.reshape(batch * seq, d)
    for layer in range(depth):
        j = layer // 2
        if layer % 2 == 0:
            xf = _att_layer(xf, norm_g[layer], att_w_in[j], att_q_norm_g[j], att_kv_norm_g[j],
                            att_w_uq[j], att_w_ukv[j], att_w_out[j], tables, batch, seq, topk)
        else:
            xf = _sgu_layer(xf, norm_g[layer], sgu_w_in[j], sgu_ln_g[j], sgu_ln_b[j], sgu_w_s[j],
                            sgu_b_s[j], sgu_w_out[j])
    return _rmsnorm(xf, final_norm_g, x.dtype).reshape(batch, seq, d)
```

```python
import functools
import math

import jax
import jax.numpy as jnp
from jax import lax
from jax.experimental import pallas as pl
from jax.experimental.pallas import tpu as pltpu

A_HEADS = 16
A_HEAD_DIM = 128
IDX_HEADS = 32
IDX_DIM = 64
TOPK_MAX = 256
B_HEADS = 16
QK_NOPE_DIM = 128
QK_ROPE_DIM = 64
V_HEAD_DIM = 128
ROPE_THETA = 10000.0
CHUNK = 128
EPS = 1e-6

LANES = 128
V7X_VMEM_BYTES = 64 * 1024 * 1024
VMEM_BUDGET = V7X_VMEM_BYTES - 8 * 1024 * 1024
COMPILER_SCRATCH_BYTES = 2 * 1024 * 1024

MLA_QK_PAD = 2 * LANES
NEG_BIG = -1e30
INT_MIN = -(2 ** 31)
INT16_MIN = -(2 ** 15)
LOG2E = math.log2(math.e)

MM_BLOCK_ELEMS = 4 * 1024 * 1024
MM_WIDE_TILE = 1024
DSA_TQ = 128
DSA_TK = 512
DSA_SCORE_ROWS = 512
MLA_T = 512
MLA_HEADS_PER_STEP = 2

F32 = jnp.float32
BF16 = jnp.bfloat16


def _cparams(sem, vmem_bytes):
    limit = int(min(max(vmem_bytes + COMPILER_SCRATCH_BYTES, 16 * 1024 * 1024), VMEM_BUDGET))
    return pltpu.CompilerParams(dimension_semantics=sem, vmem_limit_bytes=limit)


def _pick(n, candidates):
    for c in candidates:
        if n % c == 0:
            return c
    raise ValueError(f"no tile in {candidates} divides {n}")


def _nt_dot(a, b):
    return lax.dot_general(a, b, (((1,), (1,)), ((), ())), preferred_element_type=F32)


def _rmsnorm_kernel(x_ref, g_ref, o_ref):
    x = x_ref[...]
    ms = jnp.mean(x * x, axis=-1, keepdims=True)
    o_ref[...] = (x * lax.rsqrt(ms + EPS) * g_ref[...]).astype(o_ref.dtype)


def _rmsnorm(x, g, out_dtype):
    n, d = x.shape
    tm = _pick(n, (256, 128, 64, 32, 16, 8))
    return pl.pallas_call(
        _rmsnorm_kernel,
        out_shape=jax.ShapeDtypeStruct((n, d), out_dtype),
        grid=(n // tm,),
        in_specs=[pl.BlockSpec((tm, d), lambda i: (i, 0)),
                  pl.BlockSpec((1, d), lambda i: (0, 0))],
        out_specs=pl.BlockSpec((tm, d), lambda i: (i, 0)),
        compiler_params=_cparams(("parallel",), 6 * tm * d * 4),
        name="rmsnorm",
    )(x, g.reshape(1, d))


def _mm_kernel(*refs, nk, n_extra, epilogue):
    a_ref, b_ref = refs[0], refs[1]
    extra = refs[2:2 + n_extra]
    o_ref = refs[2 + n_extra]

    def finish(acc):
        vals = [e[...] for e in extra]
        o_ref[...] = epilogue(acc, *vals).astype(o_ref.dtype)

    if nk == 1:
        finish(jnp.dot(a_ref[...], b_ref[...], preferred_element_type=F32))
        return

    acc_ref = refs[3 + n_extra]
    k = pl.program_id(2)

    @pl.when(k == 0)
    def _():
        acc_ref[...] = jnp.zeros_like(acc_ref)

    acc_ref[...] += jnp.dot(a_ref[...], b_ref[...], preferred_element_type=F32)

    @pl.when(k == nk - 1)
    def _():
        finish(acc_ref[...])


def _identity_epilogue(acc):
    return acc


def _residual_epilogue(acc, x):
    return x + acc


def _gelu_epilogue(acc):
    return jax.nn.gelu(acc)


def _silu_epilogue(acc):
    return jax.nn.silu(acc)


def _matmul(a, b, out_dtype, *, layer=None, cols=None, epilogue=_identity_epilogue, extras=(), name="matmul"):
    m, kdim = a.shape
    col0, n = (0, b.shape[-1]) if cols is None else cols
    tm = _pick(m, (1024, 512, 256, 128))
    tn = _pick(n, (MM_WIDE_TILE, 512, 256, 128))
    budget = MM_BLOCK_ELEMS if kdim * tn <= MM_BLOCK_ELEMS else MM_BLOCK_ELEMS // 2
    tk = _pick(kdim, tuple(c for c in (4096, 2048, 1024, 512, 256, 128) if c * tn <= budget))
    nk = kdim // tk
    assert col0 % tn == 0
    jb = col0 // tn
    out_bytes = jnp.dtype(out_dtype).itemsize
    vmem = 2 * (tm * tk * 2 + tk * tn * 2 + tm * tn * out_bytes)
    vmem += sum(2 * tm * tn * e.dtype.itemsize for e in extras)
    vmem += 3 * tm * tn * 4
    scratch = [pltpu.VMEM((tm, tn), F32)] if nk > 1 else []
    if layer is None:
        b_spec = pl.BlockSpec((tk, tn), lambda i, j, k: (k, jb + j))
    else:
        b_spec = pl.BlockSpec((None, tk, tn), lambda i, j, k: (layer, k, jb + j))
    return pl.pallas_call(
        functools.partial(_mm_kernel, nk=nk, n_extra=len(extras), epilogue=epilogue),
        out_shape=jax.ShapeDtypeStruct((m, n), out_dtype),
        grid=(m // tm, n // tn, nk),
        in_specs=[pl.BlockSpec((tm, tk), lambda i, j, k: (i, k)), b_spec]
                 + [pl.BlockSpec((tm, tn), lambda i, j, k: (i, j)) for _ in extras],
        out_specs=pl.BlockSpec((tm, tn), lambda i, j, k: (i, j)),
        scratch_shapes=scratch,
        compiler_params=_cparams(("parallel", "parallel", "arbitrary"), vmem),
        name=name,
    )(a, b, *extras)


def _att_out_kernel(ya_ref, yb_ref, wa_ref, wb_ref, x_ref, o_ref):
    acc = jnp.dot(ya_ref[...], wa_ref[...], preferred_element_type=F32)
    acc += jnp.dot(yb_ref[...], wb_ref[...], preferred_element_type=F32)
    o_ref[...] = x_ref[...] + acc


def _att_out(ya, yb, w, layer, x):
    n, ka = ya.shape
    kb = yb.shape[1]
    assert ka == kb
    d = w.shape[-1]
    tm = _pick(n, (1024, 512, 256, 128))
    tn = _pick(d, (512, 256, 128))
    vmem = 2 * (tm * (ka + kb) * 2 + (ka + kb) * tn * 2 + 2 * tm * tn * 4) + 3 * tm * tn * 4
    return pl.pallas_call(
        _att_out_kernel,
        out_shape=jax.ShapeDtypeStruct((n, d), F32),
        grid=(n // tm, d // tn),
        in_specs=[pl.BlockSpec((tm, ka), lambda i, j: (i, 0)),
                  pl.BlockSpec((tm, kb), lambda i, j: (i, 0)),
                  pl.BlockSpec((None, ka, tn), lambda i, j: (layer, 0, j)),
                  pl.BlockSpec((None, kb, tn), lambda i, j: (layer, 1, j)),
                  pl.BlockSpec((tm, tn), lambda i, j: (i, j))],
        out_specs=pl.BlockSpec((tm, tn), lambda i, j: (i, j)),
        compiler_params=_cparams(("parallel", "parallel"), vmem),
        name="att_out",
    )(ya, yb, w, w, x)


def _rope_lanes(v, tc, tsm, tsp):
    half = QK_ROPE_DIM // 2
    return (v * tc
            + pltpu.roll(v, LANES - half, axis=1) * tsm
            + pltpu.roll(v, half, axis=1) * tsp)


def _mla_q_kernel(c_ref, g_ref, w_ref, tc_ref, tsm_ref, tsp_ref, o_ref, *, heads):
    c = c_ref[...]
    ms = jnp.mean(c * c, axis=-1, keepdims=True)
    cn = (c * lax.rsqrt(ms + EPS) * g_ref[...]).astype(BF16)
    q = jnp.dot(cn, w_ref[...], preferred_element_type=F32)
    tc, tsm, tsp = tc_ref[...], tsm_ref[...], tsp_ref[...]
    for h in range(heads):
        base = h * MLA_QK_PAD
        o_ref[h, :, :LANES] = q[:, base:base + LANES].astype(BF16)
        r = q[:, base + LANES:base + 2 * LANES]
        o_ref[h, :, LANES:] = _rope_lanes(r, tc, tsm, tsp).astype(BF16)


def _mla_q(proj32, col_block, g, w, tc, tsm, tsp):
    n = proj32.shape[0]
    rank, width = w.shape
    heads = width // MLA_QK_PAD
    tm = _pick(n, (512, 256, 128))
    vmem = 2 * (tm * rank * 4 + rank * width * 2 + 3 * tm * LANES * 4 + tm * width * 2) + 2 * tm * width * 4
    row = lambda i: (i, 0)
    return pl.pallas_call(
        functools.partial(_mla_q_kernel, heads=heads),
        out_shape=jax.ShapeDtypeStruct((heads, n, MLA_QK_PAD), BF16),
        grid=(n // tm,),
        in_specs=[pl.BlockSpec((tm, rank), lambda i: (i, col_block)),
                  pl.BlockSpec((1, rank), lambda i: (0, 0)),
                  pl.BlockSpec((rank, width), lambda i: (0, 0)),
                  pl.BlockSpec((tm, LANES), row),
                  pl.BlockSpec((tm, LANES), row),
                  pl.BlockSpec((tm, LANES), row)],
        out_specs=pl.BlockSpec((heads, tm, MLA_QK_PAD), lambda i: (0, i, 0)),
        compiler_params=_cparams(("parallel",), vmem),
        name="mla_q",
    )(proj32, g.reshape(1, rank), w, tc, tsm, tsp)


def _ones_column(shape, dtype):
    lane = lax.broadcasted_iota(jnp.int32, shape, 1)
    return jnp.where(lane == 0, 1.0, 0.0).astype(dtype)


def _mla_kv_kernel(c_ref, g_ref, w_ref, kr_ref, tc_ref, tsm_ref, tsp_ref, k_ref, v_ref, *, heads):
    c = c_ref[...]
    ms = jnp.mean(c * c, axis=-1, keepdims=True)
    cn = (c * lax.rsqrt(ms + EPS) * g_ref[...]).astype(BF16)
    kv = jnp.dot(cn, w_ref[...], preferred_element_type=F32)
    kr = _rope_lanes(kr_ref[...], tc_ref[...], tsm_ref[...], tsp_ref[...]).astype(BF16)
    ones_col = _ones_column(kr.shape, BF16)
    for h in range(heads):
        k_ref[h, :, :LANES] = kv[:, h * LANES:(h + 1) * LANES].astype(BF16)
        k_ref[h, :, LANES:] = kr
        v_ref[h, :, :LANES] = kv[:, (heads + h) * LANES:(heads + h + 1) * LANES].astype(BF16)
        v_ref[h, :, LANES:] = ones_col


def _mla_kv(proj32, c_block, kr_block, g, w, tc, tsm, tsp):
    n = proj32.shape[0]
    rank, width = w.shape
    heads = width // (2 * LANES)
    tm = _pick(n, (512, 256, 128))
    vmem = 2 * (tm * rank * 4 + rank * width * 2 + 4 * tm * LANES * 4 + 2 * tm * heads * MLA_QK_PAD * 2)
    vmem += 2 * tm * width * 4
    row = lambda i: (i, 0)
    return pl.pallas_call(
        functools.partial(_mla_kv_kernel, heads=heads),
        out_shape=(jax.ShapeDtypeStruct((heads, n, MLA_QK_PAD), BF16),
                   jax.ShapeDtypeStruct((heads, n, MLA_QK_PAD), BF16)),
        grid=(n // tm,),
        in_specs=[pl.BlockSpec((tm, rank), lambda i: (i, c_block)),
                  pl.BlockSpec((1, rank), lambda i: (0, 0)),
                  pl.BlockSpec((rank, width), lambda i: (0, 0)),
                  pl.BlockSpec((tm, LANES), lambda i: (i, kr_block)),
                  pl.BlockSpec((tm, LANES), row),
                  pl.BlockSpec((tm, LANES), row),
                  pl.BlockSpec((tm, LANES), row)],
        out_specs=(pl.BlockSpec((heads, tm, MLA_QK_PAD), lambda i: (0, i, 0)),
                   pl.BlockSpec((heads, tm, MLA_QK_PAD), lambda i: (0, i, 0))),
        compiler_params=_cparams(("parallel",), vmem),
        name="mla_kv",
    )(proj32, g.reshape(1, rank), w, proj32, tc, tsm, tsp)


def _flash_init(p_ref, a_ref, m_ref, acc_ref):
    p_ref[...] = jnp.zeros_like(p_ref)
    a_ref[...] = jnp.ones_like(a_ref)
    m_ref[...] = jnp.full_like(m_ref, NEG_BIG)
    acc_ref[...] = jnp.zeros_like(acc_ref)


def _flash_softmax(s, p_ref, a_ref, m_ref, scale, rows=slice(None)):
    m_old = m_ref[rows, :]
    m_new = jnp.maximum(m_old, jnp.max(s, axis=-1, keepdims=True))
    a_ref[rows, :] = jnp.exp2((m_old - m_new) * scale)
    p_ref[rows, :] = jnp.exp2((s - m_new) * scale).astype(p_ref.dtype)
    m_ref[rows, :] = m_new


def _mla_flash_kernel(q_ref, k_ref, v_ref, gate_ref, o_ref, s_ref, p_ref, a_ref, m_ref, acc_ref,
                      *, t, hg, scale):
    i = pl.program_id(2)
    _flash_init(p_ref, a_ref, m_ref, acc_ref)
    head_rows = [slice(g * t, (g + 1) * t) for g in range(hg)]

    def scores(c):
        off = pl.multiple_of(c * t, t)
        for g, rows in enumerate(head_rows):
            s_ref[rows, :] = _nt_dot(q_ref[g], k_ref[g, pl.ds(off, t), :])

    def accumulate(c):
        off = pl.multiple_of(c * t, t)
        for g, rows in enumerate(head_rows):
            acc_ref[rows, :] = a_ref[rows, :] * acc_ref[rows, :] + jnp.dot(
                p_ref[rows, :], v_ref[g, pl.ds(off, t), :], preferred_element_type=F32)

    def body(c, carry):
        accumulate(jnp.maximum(c - 1, 0))
        _flash_softmax(s_ref[...], p_ref, a_ref, m_ref, scale)
        scores(c + 1)
        return carry

    scores(0)
    lax.fori_loop(0, i, body, 0)
    accumulate(jnp.maximum(i - 1, 0))
    row = lax.broadcasted_iota(jnp.int32, (t, t), 0)
    col = lax.broadcasted_iota(jnp.int32, (t, t), 1)
    for rows in head_rows:
        _flash_softmax(jnp.where(col <= row, s_ref[rows, :], NEG_BIG), p_ref, a_ref, m_ref, scale, rows)
    accumulate(i)
    for g, rows in enumerate(head_rows):
        o = acc_ref[rows, :LANES] / acc_ref[rows, LANES:LANES + 1]
        lanes = slice(g * LANES, (g + 1) * LANES)
        o_ref[:, lanes] = (o * jax.nn.silu(gate_ref[:, lanes])).astype(o_ref.dtype)


def _mla_flash(q, k, v, proj32, gate_col0, batch, seq):
    heads = v.shape[0]
    hg = _pick(heads, (MLA_HEADS_PER_STEP, 1))
    t = _pick(seq, (MLA_T, 256, 128))
    nt = seq // t
    scale = LOG2E / math.sqrt(QK_NOPE_DIM + QK_ROPE_DIM)
    gw = hg * LANES
    assert gate_col0 % gw == 0
    vmem = 2 * hg * (t * MLA_QK_PAD * 2 + 2 * seq * MLA_QK_PAD * 2 + t * LANES * 4 + t * LANES * 2)
    vmem += hg * (t * t * 6 + t * (2 * LANES + MLA_QK_PAD) * 4 + 4 * t * t * 4)
    return pl.pallas_call(
        functools.partial(_mla_flash_kernel, t=t, hg=hg, scale=scale),
        out_shape=jax.ShapeDtypeStruct((batch * seq, heads * LANES), BF16),
        grid=(batch, heads // hg, nt),
        in_specs=[pl.BlockSpec((hg, t, MLA_QK_PAD), lambda b, h, i: (h, b * nt + i, 0)),
                  pl.BlockSpec((hg, seq, MLA_QK_PAD), lambda b, h, i: (h, b, 0)),
                  pl.BlockSpec((hg, seq, MLA_QK_PAD), lambda b, h, i: (h, b, 0)),
                  pl.BlockSpec((t, gw), lambda b, h, i: (b * nt + i, gate_col0 // gw + h))],
        out_specs=pl.BlockSpec((t, gw), lambda b, h, i: (b * nt + i, h)),
        scratch_shapes=[pltpu.VMEM((hg * t, t), F32), pltpu.VMEM((hg * t, t), BF16),
                        pltpu.VMEM((hg * t, 1), F32), pltpu.VMEM((hg * t, 1), F32),
                        pltpu.VMEM((hg * t, MLA_QK_PAD), F32)],
        compiler_params=_cparams(("parallel", "parallel", "arbitrary"), vmem),
        name="mla_flash",
    )(q, k, v, proj32)


def _order_key(score):
    score = jnp.where(score == 0.0, 0.0, score)
    bits = pltpu.bitcast(score, jnp.int32)
    return jnp.where(bits >= 0, bits, bits ^ jnp.int32(0x7FFFFFFF))


def _dsa_kernel(qa_ref, qi_ref, wi_ref, gate_ref, ka_ref, va_ref, k1_ref, k2_ref, o_ref,
                keys_ref, half_ref, qall_ref, s_ref, p_ref, a_ref, m_ref, acc_ref, tri_ref, vone_ref,
                *, tq, tk, sb, heads, idx_heads, topk, wi_scale, scale):
    i = pl.program_id(1)
    t0 = i * tq
    n_chunks = (t0 + tq + tk - 1) // tk
    row_pos = t0 + lax.broadcasted_iota(jnp.int32, (tq, tk), 0)
    col_iota = lax.broadcasted_iota(jnp.int32, (tq, tk), 1)

    w = wi_ref[...] * wi_scale

    def score_chunk(c, carry):
        off = pl.multiple_of(c * tk, tk)
        k1 = k1_ref[pl.ds(off, tk), :]
        k2 = k2_ref[pl.ds(off, tk), :]
        acc = jnp.zeros((tq, tk), F32)
        for p in range(idx_heads // 2):
            qp = qi_ref[:, p * LANES:(p + 1) * LANES]
            acc += jnp.maximum(_nt_dot(qp, k1), 0.0) * w[:, 2 * p:2 * p + 1]
            acc += jnp.maximum(_nt_dot(qp, k2), 0.0) * w[:, 2 * p + 1:2 * p + 2]
        key = jnp.where(off + col_iota <= row_pos, _order_key(acc), INT_MIN)
        keys_ref[:, pl.ds(off, tk)] = key
        half_ref[:, pl.ds(off, tk)] = jnp.right_shift(key, 16).astype(jnp.int16)
        return carry

    lax.fori_loop(0, n_chunks, score_chunk, 0)

    def count16(pred):
        def body(c, cnt):
            x = half_ref[:, pl.ds(pl.multiple_of(c * tk, tk), tk)]
            hit = jnp.where(pred(x), jnp.int16(1), jnp.int16(0))
            for u in range(tk // LANES):
                cnt = cnt + hit[:, u * LANES:(u + 1) * LANES]
            return cnt
        cnt = lax.fori_loop(0, n_chunks, body, jnp.zeros((tq, LANES), jnp.int16))
        return jnp.sum(cnt.astype(F32), axis=-1, keepdims=True)

    def kth_largest16(target):
        def enough(cand):
            c16 = cand.astype(jnp.int16)
            return count16(lambda x: x >= c16) >= target
        t = jnp.where(enough(jnp.zeros((tq, 1), jnp.int32)), 0, INT16_MIN).astype(jnp.int32)

        def bit_step(b, t):
            cand = t + jnp.left_shift(jnp.int32(1), 14 - b)
            return jnp.where(enough(cand), cand, t)
        return lax.fori_loop(0, 15, bit_step, t)

    kf = float(topk)
    t_hi = kth_largest16(kf)
    t_hi16 = t_hi.astype(jnp.int16)
    n_gt_hi = count16(lambda x: x > t_hi16)

    def low_half_chunk(c, carry):
        off = pl.multiple_of(c * tk, tk)
        key = keys_ref[:, pl.ds(off, tk)]
        low = jnp.bitwise_and(key, 0xFFFF) + INT16_MIN
        low = jnp.where(jnp.right_shift(key, 16) == t_hi, low, INT16_MIN)
        half_ref[:, pl.ds(off, tk)] = low.astype(jnp.int16)
        return carry

    lax.fori_loop(0, n_chunks, low_half_chunk, 0)
    t_lo = kth_largest16(kf - n_gt_hi)
    thr = jnp.left_shift(t_hi, 16) + (t_lo - INT16_MIN)

    def count_gt_body(c, cnt):
        off = pl.multiple_of(c * tk, tk)
        gt = jnp.where(keys_ref[:, pl.ds(off, tk)] > thr, 1.0, 0.0)
        for u in range(tk // LANES):
            cnt = cnt + gt[:, u * LANES:(u + 1) * LANES]
        return cnt
    n_gt = jnp.sum(lax.fori_loop(0, n_chunks, count_gt_body, jnp.zeros((tq, LANES), F32)),
                   axis=-1, keepdims=True)
    need = jnp.where(thr == INT_MIN, 0.0, kf - n_gt)

    @pl.when(i == 0)
    def _():
        r = lax.broadcasted_iota(jnp.int32, (tk, tk), 0)
        cc = lax.broadcasted_iota(jnp.int32, (tk, tk), 1)
        tri_ref[...] = jnp.where(r <= cc, 1.0, 0.0).astype(tri_ref.dtype)
        vone_ref[:, :LANES] = va_ref[...]
        vone_ref[:, LANES:] = _ones_column(va_ref.shape, vone_ref.dtype)

    for h in range(heads):
        qall_ref[h * tq:(h + 1) * tq, :] = qa_ref[:, h * LANES:(h + 1) * LANES]
    _flash_init(p_ref, a_ref, m_ref, acc_ref)

    def scores(c):
        ka = ka_ref[pl.ds(pl.multiple_of(c * tk, tk), tk), :]
        for r in range(0, heads * tq, sb):
            s_ref[r:r + sb, :] = _nt_dot(qall_ref[r:r + sb, :], ka)

    def accumulate(c):
        acc_ref[...] = a_ref[...] * acc_ref[...] + jnp.dot(
            p_ref[...], vone_ref[pl.ds(pl.multiple_of(c * tk, tk), tk), :], preferred_element_type=F32)

    def softmax(c, n_eq_before):
        key = keys_ref[:, pl.ds(pl.multiple_of(c * tk, tk), tk)]
        eq = key == thr
        rank = n_eq_before + jnp.dot(jnp.where(eq, 1.0, 0.0).astype(tri_ref.dtype), tri_ref[...],
                                     preferred_element_type=F32)
        bias = jnp.where((key > thr) | (eq & (rank <= need)), 0.0, NEG_BIG)
        for h in range(heads):
            rows = slice(h * tq, (h + 1) * tq)
            _flash_softmax(s_ref[rows, :] + bias, p_ref, a_ref, m_ref, scale, rows)
        return rank[:, tk - 1:tk]

    def body(c, n_eq):
        accumulate(jnp.maximum(c - 1, 0))
        n_eq = softmax(c, n_eq)
        scores(c + 1)
        return n_eq

    scores(0)
    n_eq = lax.fori_loop(0, n_chunks - 1, body, jnp.zeros((tq, 1), F32))
    accumulate(jnp.maximum(n_chunks - 2, 0))
    softmax(n_chunks - 1, n_eq)
    accumulate(n_chunks - 1)
    for h in range(heads):
        hs = slice(h * tq, (h + 1) * tq)
        g = gate_ref[:, h * LANES:(h + 1) * LANES]
        o = acc_ref[hs, :LANES] / acc_ref[hs, LANES:LANES + 1]
        o_ref[:, h * LANES:(h + 1) * LANES] = (o * jax.nn.silu(g)).astype(o_ref.dtype)


def _dsa(proj16, proj32, lay16, lay32, batch, seq, topk):
    a_w = A_HEADS * A_HEAD_DIM
    idx_w = IDX_HEADS * IDX_DIM
    tq = _pick(seq, (DSA_TQ,))
    tk = _pick(seq, (DSA_TK, 256, 128))
    nq = seq // tq
    qa_b = lay16["q_a"] // a_w
    qi_b = lay16["qi"] // idx_w
    ka_b, va_b = lay16["k_a"] // LANES, lay16["v_a"] // LANES
    k1_b, k2_b = lay16["ki1"] // LANES, lay16["ki2"] // LANES
    wi_b = lay32["wi"] // LANES
    gate_b = lay32["gate"] // a_w
    rows = A_HEADS * tq
    sb = min(DSA_SCORE_ROWS, rows)
    vmem = 2 * (tq * a_w * 2 + tq * idx_w * 2 + tq * LANES * 4 + tq * a_w * 4 + tq * a_w * 2) + 4 * seq * LANES * 2
    vmem += tq * seq * 6 + seq * 2 * LANES * 2 + rows * (LANES * 2 + tk * 6 + 4 * LANES * 4) + tk * tk * 2
    vmem += rows * tk * 4 + 8 * tq * tk * 4
    vmem = max(vmem, VMEM_BUDGET)
    once = pl.Buffered(1)
    qrow = lambda blk: (lambda b, i: (b * nq + i, blk))
    kcol = lambda blk: (lambda b, i: (b, blk))
    return pl.pallas_call(
        functools.partial(_dsa_kernel, tq=tq, tk=tk, sb=sb, heads=A_HEADS, idx_heads=IDX_HEADS, topk=topk,
                          wi_scale=IDX_HEADS ** -0.5 * IDX_DIM ** -0.5,
                          scale=LOG2E / math.sqrt(A_HEAD_DIM)),
        out_shape=jax.ShapeDtypeStruct((batch * seq, a_w), BF16),
        grid=(batch, nq),
        in_specs=[pl.BlockSpec((tq, a_w), qrow(qa_b)),
                  pl.BlockSpec((tq, idx_w), qrow(qi_b)),
                  pl.BlockSpec((tq, LANES), qrow(wi_b)),
                  pl.BlockSpec((tq, a_w), qrow(gate_b)),
                  pl.BlockSpec((seq, LANES), kcol(ka_b), pipeline_mode=once),
                  pl.BlockSpec((seq, LANES), kcol(va_b), pipeline_mode=once),
                  pl.BlockSpec((seq, LANES), kcol(k1_b), pipeline_mode=once),
                  pl.BlockSpec((seq, LANES), kcol(k2_b), pipeline_mode=once)],
        out_specs=pl.BlockSpec((tq, a_w), lambda b, i: (b * nq + i, 0)),
        scratch_shapes=[pltpu.VMEM((tq, seq), jnp.int32),
                        pltpu.VMEM((tq, seq), jnp.int16),
                        pltpu.VMEM((rows, LANES), BF16),
                        pltpu.VMEM((rows, tk), F32),
                        pltpu.VMEM((rows, tk), BF16),
                        pltpu.VMEM((rows, 1), F32),
                        pltpu.VMEM((rows, 1), F32),
                        pltpu.VMEM((rows, 2 * LANES), F32),
                        pltpu.VMEM((tk, tk), BF16),
                        pltpu.VMEM((seq, 2 * LANES), BF16)],
        compiler_params=_cparams(("parallel", "arbitrary"), vmem),
        name="dsa",
    )(proj16, proj16, proj32, proj32, proj16, proj16, proj16, proj16)


def _sgu_mix_kernel(u_ref, v_ref, sg_ref, lg_ref, lb_ref, ws_ref, bs_ref, o_ref, *, groups):
    v = v_ref[...].astype(F32)
    mu = jnp.mean(v, axis=-1, keepdims=True)
    d = v - mu
    var = jnp.mean(d * d, axis=-1, keepdims=True)
    vln = (d * lax.rsqrt(var + EPS) * lg_ref[...] + lb_ref[...]).astype(BF16)
    chunk, width = v.shape
    gw = width // groups
    row = lax.broadcasted_iota(jnp.int32, (chunk, chunk), 0)
    col = lax.broadcasted_iota(jnp.int32, (chunk, chunk), 1)
    bs = bs_ref[...]
    for g in range(groups):
        ws = jnp.where(col <= row, ws_ref[g], 0.0).astype(BF16)
        mixed = jnp.dot(ws, vln[:, g * gw:(g + 1) * gw], preferred_element_type=F32) + bs[:, g:g + 1]
        sl = slice(g * gw, (g + 1) * gw)
        y = u_ref[:, sl].astype(F32) * mixed * sg_ref[:, sl].astype(F32)
        o_ref[:, sl] = y.astype(o_ref.dtype)


def _sgu_mix(u, v, sg, ln_g, ln_b, w_s, b_s_t):
    n, width = v.shape
    groups = w_s.shape[0]
    vmem = 2 * (CHUNK * width * (2 + 2 + 2 + 2) + 2 * width * 4 + groups * CHUNK * CHUNK * 4) + 4 * CHUNK * width * 4
    row = lambda i: (i, 0)
    fixed2 = lambda i: (0, 0)
    return pl.pallas_call(
        functools.partial(_sgu_mix_kernel, groups=groups),
        out_shape=jax.ShapeDtypeStruct((n, width), BF16),
        grid=(n // CHUNK,),
        in_specs=[pl.BlockSpec((CHUNK, width), row),
                  pl.BlockSpec((CHUNK, width), row),
                  pl.BlockSpec((CHUNK, width), row),
                  pl.BlockSpec((1, width), fixed2),
                  pl.BlockSpec((1, width), fixed2),
                  pl.BlockSpec((groups, CHUNK, CHUNK), lambda i: (0, 0, 0)),
                  pl.BlockSpec((CHUNK, groups), fixed2)],
        out_specs=pl.BlockSpec((CHUNK, width), row),
        compiler_params=_cparams(("parallel",), vmem),
        name="sgu_mix",
    )(u, v, sg, ln_g.reshape(1, width), ln_b.reshape(1, width), w_s, b_s_t)


def _att_in_weights(w_in, q_rank, kv_rank):
    a_w, idx_w = A_HEADS * A_HEAD_DIM, IDX_HEADS * IDX_DIM
    att_w = a_w + B_HEADS * V_HEAD_DIM
    sizes = (a_w, A_HEAD_DIM, A_HEAD_DIM, idx_w, IDX_DIM, IDX_HEADS, q_rank, kv_rank, QK_ROPE_DIM, att_w)
    assert sum(sizes) == w_in.shape[1]
    names = ("q_a", "k_a", "v_a", "qi", "ki", "wi", "c_q", "c_kv", "k_rope", "gate")
    seg, off = {}, 0
    for nm, sz in zip(names, sizes):
        seg[nm] = w_in[:, off:off + sz]
        off += sz

    def pad_to(wseg, width, left=0):
        return jnp.pad(wseg, ((0, 0), (left, width - left - wseg.shape[1])))

    def pack(parts):
        parts = sorted(parts, key=lambda p: -p[1].shape[1])
        lay, off, cols = {}, 0, []
        for nm, wseg in parts:
            assert off % wseg.shape[1] == 0
            lay[nm] = off
            off += wseg.shape[1]
            cols.append(wseg)
        return jnp.concatenate(cols, axis=1).astype(BF16), lay

    w16, lay16 = pack([("q_a", seg["q_a"]), ("qi", seg["qi"]), ("k_a", seg["k_a"]), ("v_a", seg["v_a"]),
                       ("ki1", pad_to(seg["ki"], LANES)), ("ki2", pad_to(seg["ki"], LANES, left=IDX_DIM))])
    w32, lay32 = pack([("gate", seg["gate"]), ("c_q", seg["c_q"]), ("c_kv", seg["c_kv"]),
                       ("k_rope", pad_to(seg["k_rope"], LANES)), ("wi", pad_to(seg["wi"], LANES))])
    w32 = pad_to(w32, -(-w32.shape[1] // MM_WIDE_TILE) * MM_WIDE_TILE)
    return w16, lay16, w32, lay32


def _rope_tables(positions):
    half = QK_ROPE_DIM // 2
    inv_freq = ROPE_THETA ** (-jnp.arange(0, QK_ROPE_DIM, 2, dtype=F32) / QK_ROPE_DIM)
    ang = positions.reshape(-1).astype(F32)[:, None] * inv_freq
    cos, sin = jnp.cos(ang), jnp.sin(ang)
    z = jnp.zeros_like(cos)
    pad = jnp.zeros((cos.shape[0], LANES - 2 * half), F32)
    tc = jnp.concatenate([cos, cos, pad], axis=1)
    tsm = jnp.concatenate([-sin, z, pad], axis=1)
    tsp = jnp.concatenate([z, sin, pad], axis=1)
    return tc, tsm, tsp


def _att_layer(x, g, w_in, q_norm_g, kv_norm_g, w_uq, w_ukv, w_out16, layer, tables, batch, seq, topk):
    tc, tsm, tsp = tables
    q_rank, kv_rank = q_norm_g.shape[0], kv_norm_g.shape[0]
    w16, lay16, w32, lay32 = _att_in_weights(w_in, q_rank, kv_rank)
    h = _rmsnorm(x, g, BF16)
    proj16 = _matmul(h, w16, BF16, name="att_in16")
    proj32 = _matmul(h, w32, F32, name="att_in32")

    ya = _dsa(proj16, proj32, lay16, lay32, batch, seq, topk)

    qk = QK_NOPE_DIM + QK_ROPE_DIM
    wq = jnp.pad(w_uq.reshape(q_rank, B_HEADS, qk), ((0, 0), (0, 0), (0, MLA_QK_PAD - qk)))
    wq = wq.reshape(q_rank, B_HEADS * MLA_QK_PAD).astype(BF16)
    wkv = w_ukv.reshape(kv_rank, B_HEADS, 2, LANES).transpose(0, 2, 1, 3).reshape(kv_rank, -1).astype(BF16)
    q = _mla_q(proj32, lay32["c_q"] // q_rank, q_norm_g, wq, tc, tsm, tsp)
    k, v = _mla_kv(proj32, lay32["c_kv"] // kv_rank, lay32["k_rope"] // LANES, kv_norm_g, wkv, tc, tsm, tsp)
    yb = _mla_flash(q, k, v, proj32, lay32["gate"] + A_HEADS * A_HEAD_DIM, batch, seq)

    return _att_out(ya, yb, w_out16, layer, x)


def _sgu_layer(x, g, w_in16, w_out16, layer, ln_g, ln_b, w_s, b_s):
    width = w_in16.shape[-1] // 3
    h = _rmsnorm(x, g, BF16)
    mm = functools.partial(_matmul, h, w_in16, BF16, layer=layer)
    u = mm(cols=(0, width), epilogue=_gelu_epilogue, name="sgu_in_u")
    v = mm(cols=(width, width), epilogue=_gelu_epilogue, name="sgu_in_v")
    sg = mm(cols=(2 * width, width), epilogue=_silu_epilogue, name="sgu_in_gate")
    y = _sgu_mix(u, v, sg, ln_g, ln_b, w_s, b_s.T)
    return _matmul(y, w_out16, F32, layer=layer, epilogue=_residual_epilogue, extras=(x,), name="sgu_out")


def kernel(x, positions, norm_g, final_norm_g, att_w_in, att_q_norm_g, att_kv_norm_g, att_w_uq, att_w_ukv,
           att_w_out, sgu_w_in, sgu_ln_g, sgu_ln_b, sgu_w_s, sgu_b_s, sgu_w_out):
    batch, seq, d = x.shape
    depth = norm_g.shape[0]
    topk = min(TOPK_MAX, seq // 4)
    tables = _rope_tables(positions)
    xf = x.reshape(batch * seq, d)
    att_w_out16 = att_w_out.astype(BF16)
    sgu_w_in16, sgu_w_out16 = sgu_w_in.astype(BF16), sgu_w_out.astype(BF16)
    for layer in range(depth):
        j = layer // 2
        if layer % 2 == 0:
            xf = _att_layer(xf, norm_g[layer], att_w_in[j], att_q_norm_g[j], att_kv_norm_g[j],
                            att_w_uq[j], att_w_ukv[j], att_w_out16, j, tables, batch, seq, topk)
        else:
            xf = _sgu_layer(xf, norm_g[layer], sgu_w_in16, sgu_w_out16, j, sgu_ln_g[j], sgu_ln_b[j],
                            sgu_w_s[j], sgu_b_s[j])
    return _rmsnorm(xf, final_norm_g, x.dtype).reshape(batch, seq, d)
```

```python
import functools
import math

import jax
import jax.numpy as jnp
from jax import lax
from jax.experimental import pallas as pl
from jax.experimental.pallas import tpu as pltpu

A_HEADS = 16
A_HEAD_DIM = 128
IDX_HEADS = 32
IDX_DIM = 64
TOPK_MAX = 256
B_HEADS = 16
QK_NOPE_DIM = 128
QK_ROPE_DIM = 64
V_HEAD_DIM = 128
ROPE_THETA = 10000.0
CHUNK = 128
EPS = 1e-6

LANES = 128
V7X_VMEM_BYTES = 64 * 1024 * 1024
VMEM_BUDGET = V7X_VMEM_BYTES - 8 * 1024 * 1024
COMPILER_SCRATCH_BYTES = 2 * 1024 * 1024

MLA_QK_PAD = 2 * LANES
NEG_BIG = -1e30
INT_MIN = -(2 ** 31)
INT16_MIN = -(2 ** 15)
LOG2E = math.log2(math.e)

MM_BLOCK_ELEMS = 4 * 1024 * 1024
MM_WIDE_TILE = 1024
DSA_TQ = 128
DSA_TK = 512
DSA_SCORE_ROWS = 512
MLA_T = 512
MLA_HEADS_PER_STEP = 2

F32 = jnp.float32
BF16 = jnp.bfloat16


def _cparams(sem, vmem_bytes):
    limit = int(min(max(vmem_bytes + COMPILER_SCRATCH_BYTES, 16 * 1024 * 1024), VMEM_BUDGET))
    return pltpu.CompilerParams(dimension_semantics=sem, vmem_limit_bytes=limit)


def _pick(n, candidates):
    for c in candidates:
        if n % c == 0:
            return c
    raise ValueError(f"no tile in {candidates} divides {n}")


def _nt_dot(a, b):
    return lax.dot_general(a, b, (((1,), (1,)), ((), ())), preferred_element_type=F32)


def _rmsnorm_kernel(x_ref, g_ref, o_ref):
    x = x_ref[...]
    ms = jnp.mean(x * x, axis=-1, keepdims=True)
    o_ref[...] = (x * lax.rsqrt(ms + EPS) * g_ref[...]).astype(o_ref.dtype)


def _rmsnorm(x, g, out_dtype):
    n, d = x.shape
    tm = _pick(n, (256, 128, 64, 32, 16, 8))
    return pl.pallas_call(
        _rmsnorm_kernel,
        out_shape=jax.ShapeDtypeStruct((n, d), out_dtype),
        grid=(n // tm,),
        in_specs=[pl.BlockSpec((tm, d), lambda i: (i, 0)),
                  pl.BlockSpec((1, d), lambda i: (0, 0))],
        out_specs=pl.BlockSpec((tm, d), lambda i: (i, 0)),
        compiler_params=_cparams(("parallel",), 6 * tm * d * 4),
        name="rmsnorm",
    )(x, g.reshape(1, d))


def _mm_kernel(*refs, nk, n_extra, epilogue):
    a_ref, b_ref = refs[0], refs[1]
    extra = refs[2:2 + n_extra]
    o_ref = refs[2 + n_extra]

    def finish(acc):
        vals = [e[...] for e in extra]
        if callable(epilogue):
            o_ref[...] = epilogue(acc, *vals).astype(o_ref.dtype)
            return
        first_blocks, epi_a, epi_b = epilogue
        j = pl.program_id(1)

        @pl.when(j < first_blocks)
        def _():
            o_ref[...] = epi_a(acc, *vals).astype(o_ref.dtype)

        @pl.when(j >= first_blocks)
        def _():
            o_ref[...] = epi_b(acc, *vals).astype(o_ref.dtype)

    if nk == 1:
        finish(jnp.dot(a_ref[...], b_ref[...], preferred_element_type=F32))
        return

    acc_ref = refs[3 + n_extra]
    k = pl.program_id(2)

    @pl.when(k == 0)
    def _():
        acc_ref[...] = jnp.zeros_like(acc_ref)

    acc_ref[...] += jnp.dot(a_ref[...], b_ref[...], preferred_element_type=F32)

    @pl.when(k == nk - 1)
    def _():
        finish(acc_ref[...])


def _identity_epilogue(acc):
    return acc


def _residual_epilogue(acc, x):
    return x + acc


def _gelu_epilogue(acc):
    return jax.nn.gelu(acc)


def _silu_epilogue(acc):
    return jax.nn.silu(acc)


def _matmul(a, b, out_dtype, *, layer=None, cols=None, epilogue=_identity_epilogue, extras=(), name="matmul"):
    m, kdim = a.shape
    col0, n = (0, b.shape[-1]) if cols is None else cols
    tm = _pick(m, (1024, 512, 256, 128))
    tn = _pick(n, (MM_WIDE_TILE, 512, 256, 128))
    budget = MM_BLOCK_ELEMS if kdim * tn <= MM_BLOCK_ELEMS else MM_BLOCK_ELEMS // 2
    tk = _pick(kdim, tuple(c for c in (4096, 2048, 1024, 512, 256, 128) if c * tn <= budget))
    nk = kdim // tk
    assert col0 % tn == 0
    jb = col0 // tn
    if not callable(epilogue):
        assert epilogue[0] % tn == 0
        epilogue = (epilogue[0] // tn, epilogue[1], epilogue[2])
    out_bytes = jnp.dtype(out_dtype).itemsize
    vmem = 2 * (tm * tk * 2 + tk * tn * 2 + tm * tn * out_bytes)
    vmem += sum(2 * tm * tn * e.dtype.itemsize for e in extras)
    vmem += 3 * tm * tn * 4
    scratch = [pltpu.VMEM((tm, tn), F32)] if nk > 1 else []
    if layer is None:
        b_spec = pl.BlockSpec((tk, tn), lambda i, j, k: (k, jb + j))
    else:
        b_spec = pl.BlockSpec((None, tk, tn), lambda i, j, k: (layer, k, jb + j))
    return pl.pallas_call(
        functools.partial(_mm_kernel, nk=nk, n_extra=len(extras), epilogue=epilogue),
        out_shape=jax.ShapeDtypeStruct((m, n), out_dtype),
        grid=(m // tm, n // tn, nk),
        in_specs=[pl.BlockSpec((tm, tk), lambda i, j, k: (i, k)), b_spec]
                 + [pl.BlockSpec((tm, tn), lambda i, j, k: (i, j)) for _ in extras],
        out_specs=pl.BlockSpec((tm, tn), lambda i, j, k: (i, j)),
        scratch_shapes=scratch,
        compiler_params=_cparams(("parallel", "parallel", "arbitrary"), vmem),
        name=name,
    )(a, b, *extras)


def _att_out_kernel(ya_ref, yb_ref, wa_ref, wb_ref, x_ref, o_ref):
    acc = jnp.dot(ya_ref[...], wa_ref[...], preferred_element_type=F32)
    acc += jnp.dot(yb_ref[...], wb_ref[...], preferred_element_type=F32)
    o_ref[...] = x_ref[...] + acc


def _att_out(ya, yb, w, layer, x):
    n, ka = ya.shape
    kb = yb.shape[1]
    assert ka == kb
    d = w.shape[-1]
    tm = _pick(n, (1024, 512, 256, 128))
    tn = _pick(d, (512, 256, 128))
    vmem = 2 * (tm * (ka + kb) * 2 + (ka + kb) * tn * 2 + 2 * tm * tn * 4) + 3 * tm * tn * 4
    return pl.pallas_call(
        _att_out_kernel,
        out_shape=jax.ShapeDtypeStruct((n, d), F32),
        grid=(n // tm, d // tn),
        in_specs=[pl.BlockSpec((tm, ka), lambda i, j: (i, 0)),
                  pl.BlockSpec((tm, kb), lambda i, j: (i, 0)),
                  pl.BlockSpec((None, ka, tn), lambda i, j: (layer, 0, j)),
                  pl.BlockSpec((None, kb, tn), lambda i, j: (layer, 1, j)),
                  pl.BlockSpec((tm, tn), lambda i, j: (i, j))],
        out_specs=pl.BlockSpec((tm, tn), lambda i, j: (i, j)),
        compiler_params=_cparams(("parallel", "parallel"), vmem),
        name="att_out",
    )(ya, yb, w, w, x)


def _rope_lanes(v, tc, tsm, tsp):
    half = QK_ROPE_DIM // 2
    return (v * tc
            + pltpu.roll(v, LANES - half, axis=1) * tsm
            + pltpu.roll(v, half, axis=1) * tsp)


def _mla_q_kernel(c_ref, g_ref, w_ref, tc_ref, tsm_ref, tsp_ref, o_ref, *, heads):
    c = c_ref[...]
    ms = jnp.mean(c * c, axis=-1, keepdims=True)
    cn = (c * lax.rsqrt(ms + EPS) * g_ref[...]).astype(BF16)
    q = jnp.dot(cn, w_ref[...], preferred_element_type=F32)
    tc, tsm, tsp = tc_ref[...], tsm_ref[...], tsp_ref[...]
    for h in range(heads):
        base = h * MLA_QK_PAD
        o_ref[h, :, :LANES] = q[:, base:base + LANES].astype(BF16)
        r = q[:, base + LANES:base + 2 * LANES]
        o_ref[h, :, LANES:] = _rope_lanes(r, tc, tsm, tsp).astype(BF16)


def _mla_q(proj32, col_block, g, w, tc, tsm, tsp):
    n = proj32.shape[0]
    rank, width = w.shape
    heads = width // MLA_QK_PAD
    tm = _pick(n, (512, 256, 128))
    vmem = 2 * (tm * rank * 4 + rank * width * 2 + 3 * tm * LANES * 4 + tm * width * 2) + 2 * tm * width * 4
    row = lambda i: (i, 0)
    return pl.pallas_call(
        functools.partial(_mla_q_kernel, heads=heads),
        out_shape=jax.ShapeDtypeStruct((heads, n, MLA_QK_PAD), BF16),
        grid=(n // tm,),
        in_specs=[pl.BlockSpec((tm, rank), lambda i: (i, col_block)),
                  pl.BlockSpec((1, rank), lambda i: (0, 0)),
                  pl.BlockSpec((rank, width), lambda i: (0, 0)),
                  pl.BlockSpec((tm, LANES), row),
                  pl.BlockSpec((tm, LANES), row),
                  pl.BlockSpec((tm, LANES), row)],
        out_specs=pl.BlockSpec((heads, tm, MLA_QK_PAD), lambda i: (0, i, 0)),
        compiler_params=_cparams(("parallel",), vmem),
        name="mla_q",
    )(proj32, g.reshape(1, rank), w, tc, tsm, tsp)


def _ones_column(shape, dtype):
    lane = lax.broadcasted_iota(jnp.int32, shape, 1)
    return jnp.where(lane == 0, 1.0, 0.0).astype(dtype)


def _mla_kv_kernel(c_ref, g_ref, w_ref, kr_ref, tc_ref, tsm_ref, tsp_ref, k_ref, v_ref, *, heads):
    c = c_ref[...]
    ms = jnp.mean(c * c, axis=-1, keepdims=True)
    cn = (c * lax.rsqrt(ms + EPS) * g_ref[...]).astype(BF16)
    kv = jnp.dot(cn, w_ref[...], preferred_element_type=F32)
    kr = _rope_lanes(kr_ref[...], tc_ref[...], tsm_ref[...], tsp_ref[...]).astype(BF16)
    ones_col = _ones_column(kr.shape, BF16)
    for h in range(heads):
        k_ref[h, :, :LANES] = kv[:, h * LANES:(h + 1) * LANES].astype(BF16)
        k_ref[h, :, LANES:] = kr
        v_ref[h, :, :LANES] = kv[:, (heads + h) * LANES:(heads + h + 1) * LANES].astype(BF16)
        v_ref[h, :, LANES:] = ones_col


def _mla_kv(proj32, c_block, kr_block, g, w, tc, tsm, tsp):
    n = proj32.shape[0]
    rank, width = w.shape
    heads = width // (2 * LANES)
    tm = _pick(n, (512, 256, 128))
    vmem = 2 * (tm * rank * 4 + rank * width * 2 + 4 * tm * LANES * 4 + 2 * tm * heads * MLA_QK_PAD * 2)
    vmem += 2 * tm * width * 4
    row = lambda i: (i, 0)
    return pl.pallas_call(
        functools.partial(_mla_kv_kernel, heads=heads),
        out_shape=(jax.ShapeDtypeStruct((heads, n, MLA_QK_PAD), BF16),
                   jax.ShapeDtypeStruct((heads, n, MLA_QK_PAD), BF16)),
        grid=(n // tm,),
        in_specs=[pl.BlockSpec((tm, rank), lambda i: (i, c_block)),
                  pl.BlockSpec((1, rank), lambda i: (0, 0)),
                  pl.BlockSpec((rank, width), lambda i: (0, 0)),
                  pl.BlockSpec((tm, LANES), lambda i: (i, kr_block)),
                  pl.BlockSpec((tm, LANES), row),
                  pl.BlockSpec((tm, LANES), row),
                  pl.BlockSpec((tm, LANES), row)],
        out_specs=(pl.BlockSpec((heads, tm, MLA_QK_PAD), lambda i: (0, i, 0)),
                   pl.BlockSpec((heads, tm, MLA_QK_PAD), lambda i: (0, i, 0))),
        compiler_params=_cparams(("parallel",), vmem),
        name="mla_kv",
    )(proj32, g.reshape(1, rank), w, proj32, tc, tsm, tsp)


def _flash_init(p_ref, a_ref, m_ref, acc_ref):
    p_ref[...] = jnp.zeros_like(p_ref)
    a_ref[...] = jnp.ones_like(a_ref)
    m_ref[...] = jnp.full_like(m_ref, NEG_BIG)
    acc_ref[...] = jnp.zeros_like(acc_ref)


def _flash_softmax(s, p_ref, a_ref, m_ref, scale, rows=slice(None)):
    m_old = m_ref[rows, :]
    m_new = jnp.maximum(m_old, jnp.max(s, axis=-1, keepdims=True))
    a_ref[rows, :] = jnp.exp2((m_old - m_new) * scale)
    p_ref[rows, :] = jnp.exp2((s - m_new) * scale).astype(p_ref.dtype)
    m_ref[rows, :] = m_new


def _mla_flash_kernel(q_ref, k_ref, v_ref, gate_ref, o_ref, s_ref, p_ref, a_ref, m_ref, acc_ref,
                      *, t, hg, scale):
    i = pl.program_id(2)
    _flash_init(p_ref, a_ref, m_ref, acc_ref)
    head_rows = [slice(g * t, (g + 1) * t) for g in range(hg)]

    def scores(c):
        off = pl.multiple_of(c * t, t)
        for g, rows in enumerate(head_rows):
            s_ref[rows, :] = _nt_dot(q_ref[g], k_ref[g, pl.ds(off, t), :])

    def accumulate(c):
        off = pl.multiple_of(c * t, t)
        for g, rows in enumerate(head_rows):
            acc_ref[rows, :] = a_ref[rows, :] * acc_ref[rows, :] + jnp.dot(
                p_ref[rows, :], v_ref[g, pl.ds(off, t), :], preferred_element_type=F32)

    def body(c, carry):
        accumulate(jnp.maximum(c - 1, 0))
        _flash_softmax(s_ref[...], p_ref, a_ref, m_ref, scale)
        scores(c + 1)
        return carry

    scores(0)
    lax.fori_loop(0, i, body, 0)
    accumulate(jnp.maximum(i - 1, 0))
    row = lax.broadcasted_iota(jnp.int32, (t, t), 0)
    col = lax.broadcasted_iota(jnp.int32, (t, t), 1)
    for rows in head_rows:
        _flash_softmax(jnp.where(col <= row, s_ref[rows, :], NEG_BIG), p_ref, a_ref, m_ref, scale, rows)
    accumulate(i)
    for g, rows in enumerate(head_rows):
        o = acc_ref[rows, :LANES] / acc_ref[rows, LANES:LANES + 1]
        lanes = slice(g * LANES, (g + 1) * LANES)
        o_ref[:, lanes] = (o * jax.nn.silu(gate_ref[:, lanes])).astype(o_ref.dtype)


def _mla_flash(q, k, v, proj32, gate_col0, batch, seq):
    heads = v.shape[0]
    hg = _pick(heads, (MLA_HEADS_PER_STEP, 1))
    t = _pick(seq, (MLA_T, 256, 128))
    nt = seq // t
    scale = LOG2E / math.sqrt(QK_NOPE_DIM + QK_ROPE_DIM)
    gw = hg * LANES
    assert gate_col0 % gw == 0
    vmem = 2 * hg * (t * MLA_QK_PAD * 2 + 2 * seq * MLA_QK_PAD * 2 + t * LANES * 4 + t * LANES * 2)
    vmem += hg * (t * t * 6 + t * (2 * LANES + MLA_QK_PAD) * 4 + 4 * t * t * 4)
    return pl.pallas_call(
        functools.partial(_mla_flash_kernel, t=t, hg=hg, scale=scale),
        out_shape=jax.ShapeDtypeStruct((batch * seq, heads * LANES), BF16),
        grid=(batch, heads // hg, nt),
        in_specs=[pl.BlockSpec((hg, t, MLA_QK_PAD), lambda b, h, i: (h, b * nt + i, 0)),
                  pl.BlockSpec((hg, seq, MLA_QK_PAD), lambda b, h, i: (h, b, 0)),
                  pl.BlockSpec((hg, seq, MLA_QK_PAD), lambda b, h, i: (h, b, 0)),
                  pl.BlockSpec((t, gw), lambda b, h, i: (b * nt + i, gate_col0 // gw + h))],
        out_specs=pl.BlockSpec((t, gw), lambda b, h, i: (b * nt + i, h)),
        scratch_shapes=[pltpu.VMEM((hg * t, t), F32), pltpu.VMEM((hg * t, t), BF16),
                        pltpu.VMEM((hg * t, 1), F32), pltpu.VMEM((hg * t, 1), F32),
                        pltpu.VMEM((hg * t, MLA_QK_PAD), F32)],
        compiler_params=_cparams(("parallel", "parallel", "arbitrary"), vmem),
        name="mla_flash",
    )(q, k, v, proj32)


def _order_key(score):
    score = jnp.where(score == 0.0, 0.0, score)
    bits = pltpu.bitcast(score, jnp.int32)
    return jnp.where(bits >= 0, bits, bits ^ jnp.int32(0x7FFFFFFF))


def _dsa_kernel(qa_ref, qi_ref, wi_ref, gate_ref, ka_ref, va_ref, k1_ref, k2_ref, o_ref,
                keys_ref, half_ref, qall_ref, s_ref, p_ref, a_ref, m_ref, acc_ref, tri_ref, vone_ref,
                *, tq, tk, sb, heads, idx_heads, topk, wi_scale, scale):
    i = pl.program_id(1)
    t0 = i * tq
    n_chunks = (t0 + tq + tk - 1) // tk
    row_pos = t0 + lax.broadcasted_iota(jnp.int32, (tq, tk), 0)
    col_iota = lax.broadcasted_iota(jnp.int32, (tq, tk), 1)

    w = wi_ref[...] * wi_scale

    def score_chunk(c, carry):
        off = pl.multiple_of(c * tk, tk)
        k1 = k1_ref[pl.ds(off, tk), :]
        k2 = k2_ref[pl.ds(off, tk), :]
        acc = jnp.zeros((tq, tk), F32)
        for p in range(idx_heads // 2):
            qp = qi_ref[:, p * LANES:(p + 1) * LANES]
            acc += jnp.maximum(_nt_dot(qp, k1), 0.0) * w[:, 2 * p:2 * p + 1]
            acc += jnp.maximum(_nt_dot(qp, k2), 0.0) * w[:, 2 * p + 1:2 * p + 2]
        key = jnp.where(off + col_iota <= row_pos, _order_key(acc), INT_MIN)
        keys_ref[:, pl.ds(off, tk)] = key
        half_ref[:, pl.ds(off, tk)] = jnp.right_shift(key, 16).astype(jnp.int16)
        return carry

    lax.fori_loop(0, n_chunks, score_chunk, 0)

    def count16(pred):
        def body(c, cnt):
            x = half_ref[:, pl.ds(pl.multiple_of(c * tk, tk), tk)]
            hit = jnp.where(pred(x), jnp.int16(1), jnp.int16(0))
            for u in range(tk // LANES):
                cnt = cnt + hit[:, u * LANES:(u + 1) * LANES]
            return cnt
        cnt = lax.fori_loop(0, n_chunks, body, jnp.zeros((tq, LANES), jnp.int16))
        return jnp.sum(cnt.astype(F32), axis=-1, keepdims=True)

    def kth_largest16(target):
        def enough(cand):
            c16 = cand.astype(jnp.int16)
            return count16(lambda x: x >= c16) >= target
        t = jnp.where(enough(jnp.zeros((tq, 1), jnp.int32)), 0, INT16_MIN).astype(jnp.int32)

        def bit_step(b, t):
            cand = t + jnp.left_shift(jnp.int32(1), 14 - b)
            return jnp.where(enough(cand), cand, t)
        return lax.fori_loop(0, 15, bit_step, t)

    kf = float(topk)
    t_hi = kth_largest16(kf)
    t_hi16 = t_hi.astype(jnp.int16)
    n_gt_hi = count16(lambda x: x > t_hi16)

    def low_half_chunk(c, carry):
        off = pl.multiple_of(c * tk, tk)
        key = keys_ref[:, pl.ds(off, tk)]
        low = jnp.bitwise_and(key, 0xFFFF) + INT16_MIN
        low = jnp.where(jnp.right_shift(key, 16) == t_hi, low, INT16_MIN)
        half_ref[:, pl.ds(off, tk)] = low.astype(jnp.int16)
        return carry

    lax.fori_loop(0, n_chunks, low_half_chunk, 0)
    t_lo = kth_largest16(kf - n_gt_hi)
    thr = jnp.left_shift(t_hi, 16) + (t_lo - INT16_MIN)

    def count_gt_body(c, cnt):
        off = pl.multiple_of(c * tk, tk)
        gt = jnp.where(keys_ref[:, pl.ds(off, tk)] > thr, 1.0, 0.0)
        for u in range(tk // LANES):
            cnt = cnt + gt[:, u * LANES:(u + 1) * LANES]
        return cnt
    n_gt = jnp.sum(lax.fori_loop(0, n_chunks, count_gt_body, jnp.zeros((tq, LANES), F32)),
                   axis=-1, keepdims=True)
    need = jnp.where(thr == INT_MIN, 0.0, kf - n_gt)

    @pl.when(i == 0)
    def _():
        r = lax.broadcasted_iota(jnp.int32, (tk, tk), 0)
        cc = lax.broadcasted_iota(jnp.int32, (tk, tk), 1)
        tri_ref[...] = jnp.where(r <= cc, 1.0, 0.0).astype(tri_ref.dtype)
        vone_ref[:, :LANES] = va_ref[...]
        vone_ref[:, LANES:] = _ones_column(va_ref.shape, vone_ref.dtype)

    for h in range(heads):
        qall_ref[h * tq:(h + 1) * tq, :] = qa_ref[:, h * LANES:(h + 1) * LANES]
    _flash_init(p_ref, a_ref, m_ref, acc_ref)

    def scores(c):
        ka = ka_ref[pl.ds(pl.multiple_of(c * tk, tk), tk), :]
        for r in range(0, heads * tq, sb):
            s_ref[r:r + sb, :] = _nt_dot(qall_ref[r:r + sb, :], ka)

    def accumulate(c):
        acc_ref[...] = a_ref[...] * acc_ref[...] + jnp.dot(
            p_ref[...], vone_ref[pl.ds(pl.multiple_of(c * tk, tk), tk), :], preferred_element_type=F32)

    def softmax(c, n_eq_before):
        key = keys_ref[:, pl.ds(pl.multiple_of(c * tk, tk), tk)]
        eq = key == thr
        rank = n_eq_before + jnp.dot(jnp.where(eq, 1.0, 0.0).astype(tri_ref.dtype), tri_ref[...],
                                     preferred_element_type=F32)
        bias = jnp.where((key > thr) | (eq & (rank <= need)), 0.0, NEG_BIG)
        for h in range(heads):
            rows = slice(h * tq, (h + 1) * tq)
            _flash_softmax(s_ref[rows, :] + bias, p_ref, a_ref, m_ref, scale, rows)
        return rank[:, tk - 1:tk]

    def body(c, n_eq):
        accumulate(jnp.maximum(c - 1, 0))
        n_eq = softmax(c, n_eq)
        scores(c + 1)
        return n_eq

    scores(0)
    n_eq = lax.fori_loop(0, n_chunks - 1, body, jnp.zeros((tq, 1), F32))
    accumulate(jnp.maximum(n_chunks - 2, 0))
    softmax(n_chunks - 1, n_eq)
    accumulate(n_chunks - 1)
    for h in range(heads):
        hs = slice(h * tq, (h + 1) * tq)
        g = gate_ref[:, h * LANES:(h + 1) * LANES]
        o = acc_ref[hs, :LANES] / acc_ref[hs, LANES:LANES + 1]
        o_ref[:, h * LANES:(h + 1) * LANES] = (o * jax.nn.silu(g)).astype(o_ref.dtype)


def _dsa(proj16, proj32, lay16, lay32, batch, seq, topk):
    a_w = A_HEADS * A_HEAD_DIM
    idx_w = IDX_HEADS * IDX_DIM
    tq = _pick(seq, (DSA_TQ,))
    tk = _pick(seq, (DSA_TK, 256, 128))
    nq = seq // tq
    qa_b = lay16["q_a"] // a_w
    qi_b = lay16["qi"] // idx_w
    ka_b, va_b = lay16["k_a"] // LANES, lay16["v_a"] // LANES
    k1_b, k2_b = lay16["ki1"] // LANES, lay16["ki2"] // LANES
    wi_b = lay32["wi"] // LANES
    gate_b = lay32["gate"] // a_w
    rows = A_HEADS * tq
    sb = min(DSA_SCORE_ROWS, rows)
    vmem = 2 * (tq * a_w * 2 + tq * idx_w * 2 + tq * LANES * 4 + tq * a_w * 4 + tq * a_w * 2) + 4 * seq * LANES * 2
    vmem += tq * seq * 6 + seq * 2 * LANES * 2 + rows * (LANES * 2 + tk * 6 + 4 * LANES * 4) + tk * tk * 2
    vmem += rows * tk * 4 + 8 * tq * tk * 4
    vmem = max(vmem, VMEM_BUDGET)
    once = pl.Buffered(1)
    qrow = lambda blk: (lambda b, i: (b * nq + i, blk))
    kcol = lambda blk: (lambda b, i: (b, blk))
    return pl.pallas_call(
        functools.partial(_dsa_kernel, tq=tq, tk=tk, sb=sb, heads=A_HEADS, idx_heads=IDX_HEADS, topk=topk,
                          wi_scale=IDX_HEADS ** -0.5 * IDX_DIM ** -0.5,
                          scale=LOG2E / math.sqrt(A_HEAD_DIM)),
        out_shape=jax.ShapeDtypeStruct((batch * seq, a_w), BF16),
        grid=(batch, nq),
        in_specs=[pl.BlockSpec((tq, a_w), qrow(qa_b)),
                  pl.BlockSpec((tq, idx_w), qrow(qi_b)),
                  pl.BlockSpec((tq, LANES), qrow(wi_b)),
                  pl.BlockSpec((tq, a_w), qrow(gate_b)),
                  pl.BlockSpec((seq, LANES), kcol(ka_b), pipeline_mode=once),
                  pl.BlockSpec((seq, LANES), kcol(va_b), pipeline_mode=once),
                  pl.BlockSpec((seq, LANES), kcol(k1_b), pipeline_mode=once),
                  pl.BlockSpec((seq, LANES), kcol(k2_b), pipeline_mode=once)],
        out_specs=pl.BlockSpec((tq, a_w), lambda b, i: (b * nq + i, 0)),
        scratch_shapes=[pltpu.VMEM((tq, seq), jnp.int32),
                        pltpu.VMEM((tq, seq), jnp.int16),
                        pltpu.VMEM((rows, LANES), BF16),
                        pltpu.VMEM((rows, tk), F32),
                        pltpu.VMEM((rows, tk), BF16),
                        pltpu.VMEM((rows, 1), F32),
                        pltpu.VMEM((rows, 1), F32),
                        pltpu.VMEM((rows, 2 * LANES), F32),
                        pltpu.VMEM((tk, tk), BF16),
                        pltpu.VMEM((seq, 2 * LANES), BF16)],
        compiler_params=_cparams(("parallel", "arbitrary"), vmem),
        name="dsa",
    )(proj16, proj16, proj32, proj32, proj16, proj16, proj16, proj16)


def _sgu_mix_kernel(u_ref, v_ref, sg_ref, lg_ref, lb_ref, ws_ref, bs_ref, o_ref, *, groups):
    v = v_ref[...].astype(F32)
    mu = jnp.mean(v, axis=-1, keepdims=True)
    d = v - mu
    var = jnp.mean(d * d, axis=-1, keepdims=True)
    vln = (d * lax.rsqrt(var + EPS) * lg_ref[...] + lb_ref[...]).astype(BF16)
    chunk, width = v.shape
    gw = width // groups
    row = lax.broadcasted_iota(jnp.int32, (chunk, chunk), 0)
    col = lax.broadcasted_iota(jnp.int32, (chunk, chunk), 1)
    bs = bs_ref[...]
    for g in range(groups):
        ws = jnp.where(col <= row, ws_ref[g], 0.0).astype(BF16)
        mixed = jnp.dot(ws, vln[:, g * gw:(g + 1) * gw], preferred_element_type=F32) + bs[:, g:g + 1]
        sl = slice(g * gw, (g + 1) * gw)
        y = u_ref[:, sl].astype(F32) * mixed * sg_ref[:, sl].astype(F32)
        o_ref[:, sl] = y.astype(o_ref.dtype)


def _sgu_mix(uvg, ln_g, ln_b, w_s, b_s_t):
    n, width = uvg.shape[0], uvg.shape[1] // 3
    groups = w_s.shape[0]
    vmem = 2 * (CHUNK * width * (2 + 2 + 2 + 2) + 2 * width * 4 + groups * CHUNK * CHUNK * 4) + 4 * CHUNK * width * 4
    row = lambda i: (i, 0)
    fixed2 = lambda i: (0, 0)
    return pl.pallas_call(
        functools.partial(_sgu_mix_kernel, groups=groups),
        out_shape=jax.ShapeDtypeStruct((n, width), BF16),
        grid=(n // CHUNK,),
        in_specs=[pl.BlockSpec((CHUNK, width), row),
                  pl.BlockSpec((CHUNK, width), lambda i: (i, 1)),
                  pl.BlockSpec((CHUNK, width), lambda i: (i, 2)),
                  pl.BlockSpec((1, width), fixed2),
                  pl.BlockSpec((1, width), fixed2),
                  pl.BlockSpec((groups, CHUNK, CHUNK), lambda i: (0, 0, 0)),
                  pl.BlockSpec((CHUNK, groups), fixed2)],
        out_specs=pl.BlockSpec((CHUNK, width), row),
        compiler_params=_cparams(("parallel",), vmem),
        name="sgu_mix",
    )(uvg, uvg, uvg, ln_g.reshape(1, width), ln_b.reshape(1, width), w_s, b_s_t)


def _att_in_weights(w_in, q_rank, kv_rank):
    a_w, idx_w = A_HEADS * A_HEAD_DIM, IDX_HEADS * IDX_DIM
    att_w = a_w + B_HEADS * V_HEAD_DIM
    sizes = (a_w, A_HEAD_DIM, A_HEAD_DIM, idx_w, IDX_DIM, IDX_HEADS, q_rank, kv_rank, QK_ROPE_DIM, att_w)
    assert sum(sizes) == w_in.shape[1]
    names = ("q_a", "k_a", "v_a", "qi", "ki", "wi", "c_q", "c_kv", "k_rope", "gate")
    seg, off = {}, 0
    for nm, sz in zip(names, sizes):
        seg[nm] = w_in[:, off:off + sz]
        off += sz

    def pad_to(wseg, width, left=0):
        return jnp.pad(wseg, ((0, 0), (left, width - left - wseg.shape[1])))

    def pack(parts):
        parts = sorted(parts, key=lambda p: -p[1].shape[1])
        lay, off, cols = {}, 0, []
        for nm, wseg in parts:
            assert off % wseg.shape[1] == 0
            lay[nm] = off
            off += wseg.shape[1]
            cols.append(wseg)
        return jnp.concatenate(cols, axis=1).astype(BF16), lay

    w16, lay16 = pack([("q_a", seg["q_a"]), ("qi", seg["qi"]), ("k_a", seg["k_a"]), ("v_a", seg["v_a"]),
                       ("ki1", pad_to(seg["ki"], LANES)), ("ki2", pad_to(seg["ki"], LANES, left=IDX_DIM))])
    w32, lay32 = pack([("gate", seg["gate"]), ("c_q", seg["c_q"]), ("c_kv", seg["c_kv"]),
                       ("k_rope", pad_to(seg["k_rope"], LANES)), ("wi", pad_to(seg["wi"], LANES))])
    w32 = pad_to(w32, -(-w32.shape[1] // MM_WIDE_TILE) * MM_WIDE_TILE)
    return w16, lay16, w32, lay32


def _rope_tables(positions):
    half = QK_ROPE_DIM // 2
    inv_freq = ROPE_THETA ** (-jnp.arange(0, QK_ROPE_DIM, 2, dtype=F32) / QK_ROPE_DIM)
    ang = positions.reshape(-1).astype(F32)[:, None] * inv_freq
    cos, sin = jnp.cos(ang), jnp.sin(ang)
    z = jnp.zeros_like(cos)
    pad = jnp.zeros((cos.shape[0], LANES - 2 * half), F32)
    tc = jnp.concatenate([cos, cos, pad], axis=1)
    tsm = jnp.concatenate([-sin, z, pad], axis=1)
    tsp = jnp.concatenate([z, sin, pad], axis=1)
    return tc, tsm, tsp


def _att_layer(x, g, w_in, q_norm_g, kv_norm_g, w_uq, w_ukv, w_out16, layer, tables, batch, seq, topk):
    tc, tsm, tsp = tables
    q_rank, kv_rank = q_norm_g.shape[0], kv_norm_g.shape[0]
    w16, lay16, w32, lay32 = _att_in_weights(w_in, q_rank, kv_rank)
    h = _rmsnorm(x, g, BF16)
    proj16 = _matmul(h, w16, BF16, name="att_in16")
    proj32 = _matmul(h, w32, F32, name="att_in32")

    ya = _dsa(proj16, proj32, lay16, lay32, batch, seq, topk)

    qk = QK_NOPE_DIM + QK_ROPE_DIM
    wq = jnp.pad(w_uq.reshape(q_rank, B_HEADS, qk), ((0, 0), (0, 0), (0, MLA_QK_PAD - qk)))
    wq = wq.reshape(q_rank, B_HEADS * MLA_QK_PAD).astype(BF16)
    wkv = w_ukv.reshape(kv_rank, B_HEADS, 2, LANES).transpose(0, 2, 1, 3).reshape(kv_rank, -1).astype(BF16)
    q = _mla_q(proj32, lay32["c_q"] // q_rank, q_norm_g, wq, tc, tsm, tsp)
    k, v = _mla_kv(proj32, lay32["c_kv"] // kv_rank, lay32["k_rope"] // LANES, kv_norm_g, wkv, tc, tsm, tsp)
    yb = _mla_flash(q, k, v, proj32, lay32["gate"] + A_HEADS * A_HEAD_DIM, batch, seq)

    return _att_out(ya, yb, w_out16, layer, x)


def _sgu_layer(x, g, w_in16, w_out16, layer, ln_g, ln_b, w_s, b_s):
    width = w_in16.shape[-1] // 3
    h = _rmsnorm(x, g, BF16)
    uvg = _matmul(h, w_in16, BF16, layer=layer, epilogue=(2 * width, _gelu_epilogue, _silu_epilogue),
                  name="sgu_in")
    y = _sgu_mix(uvg, ln_g, ln_b, w_s, b_s.T)
    return _matmul(y, w_out16, F32, layer=layer, epilogue=_residual_epilogue, extras=(x,), name="sgu_out")


def kernel(x, positions, norm_g, final_norm_g, att_w_in, att_q_norm_g, att_kv_norm_g, att_w_uq, att_w_ukv,
           att_w_out, sgu_w_in, sgu_ln_g, sgu_ln_b, sgu_w_s, sgu_b_s, sgu_w_out):
    batch, seq, d = x.shape
    depth = norm_g.shape[0]
    topk = min(TOPK_MAX, seq // 4)
    tables = _rope_tables(positions)
    xf = x.reshape(batch * seq, d)
    att_w_out16 = att_w_out.astype(BF16)
    sgu_w_in16, sgu_w_out16 = sgu_w_in.astype(BF16), sgu_w_out.astype(BF16)
    for layer in range(depth):
        j = layer // 2
        if layer % 2 == 0:
            xf = _att_layer(xf, norm_g[layer], att_w_in[j], att_q_norm_g[j], att_kv_norm_g[j],
                            att_w_uq[j], att_w_ukv[j], att_w_out16, j, tables, batch, seq, topk)
        else:
            xf = _sgu_layer(xf, norm_g[layer], sgu_w_in16, sgu_w_out16, j, sgu_ln_g[j], sgu_ln_b[j],
                            sgu_w_s[j], sgu_b_s[j])
    return _rmsnorm(xf, final_norm_g, x.dtype).reshape(batch, seq, d)
```
